```python
import jax, jax.numpy as jnp
from jax import lax
import numpy as np

D_MODEL = 1024
BATCH = 32
SEQ = 256
DEPTH = 2
DEC_BATCH = 4
DEC_SEQ = 1024
PAST_LEN = 256

GRID_W = 64
D_A = 512
LRU_BLOCKS = 8
LRU_BW = D_A // LRU_BLOCKS
CONV_W = 4
CONV_LEFT = 1
LRU_C = 8.0
D_B = 512
HG_HEADS = 4
HG_DK = D_B // HG_HEADS
HG_DV = D_B // HG_HEADS
HG_CHUNK = 32
F_MIN = 1e-20
D_C = 512
RET_HEADS = 4
RET_DK = D_C // RET_HEADS
RET_DV = D_C // RET_HEADS
RET_CHUNK = 64
ROPE_BASE = 10000.0
D_MIX = D_A + D_B + D_C
IN_WIDTHS = (D_A, D_A, D_B, D_B, D_B, D_B, D_B, D_C, D_C, D_C, D_C)
D_IN = 2 * D_A + 5 * D_B + 4 * D_C
EPS = 1e-6

kernel_name = "hybrid_lru_hgrn2_retention_diffusion_step"

F32 = jnp.float32


def _rmsnorm(x, g):
    xf = x.astype(F32)
    y = xf * lax.rsqrt(jnp.mean(xf * xf, axis=-1, keepdims=True) + EPS)
    return (y * g.astype(F32)).astype(x.dtype)


def _head_rmsnorm(y, g, n_heads):
    bn, ln, w = y.shape
    yh = y.reshape(bn, ln, n_heads, w // n_heads)
    yh = yh * lax.rsqrt(jnp.mean(yh * yh, axis=-1, keepdims=True) + EPS)
    return (yh * g.astype(F32).reshape(n_heads, -1)).reshape(bn, ln, w)


def _comb(e1, e2):
    a1, b1 = e1
    a2, b2 = e2
    return a1 * a2, a2 * b1 + b2


def _lin_scan(a, b, h0, reverse):
    if reverse:
        a = jnp.flip(a, 1)
        b = jnp.flip(b, 1)
    a_cum, b_cum = lax.associative_scan(_comb, (a, b), axis=1)
    h = a_cum * h0[:, None, :] + b_cum
    h_fin = h[:, -1]
    if reverse:
        h = jnp.flip(h, 1)
    return h, h_fin


def _rg_lru(xa, conv_w, conv_b, wa, ba, wx, bx, lam, h0):
    bn, ln, _ = xa.shape
    xp = jnp.pad(xa, ((0, 0), (CONV_LEFT, CONV_W - 1 - CONV_LEFT), (0, 0)))
    u = conv_b
    for k in range(CONV_W):
        u = u + xp[:, k:k + ln] * conv_w[k]
    uf = u.astype(F32)
    ub = uf.reshape(bn, ln, LRU_BLOCKS, LRU_BW)
    y = jnp.zeros_like(uf)
    fins = []
    for d in range(2):
        r = jax.nn.sigmoid(jnp.einsum('blhi,hij->blhj', ub, wa[d].astype(F32)).reshape(bn, ln, D_A) + ba[d].astype(F32))
        ig = jax.nn.sigmoid(jnp.einsum('blhi,hij->blhj', ub, wx[d].astype(F32)).reshape(bn, ln, D_A) + bx[d].astype(F32))
        log_a = -LRU_C * r * jax.nn.softplus(-lam[d].astype(F32))
        a = jnp.exp(log_a)
        b = jnp.sqrt(-jnp.expm1(2.0 * log_a)) * (ig * uf)
        h, h_fin = _lin_scan(a, b, h0[:, d].astype(F32), reverse=(d == 1))
        y = y + h
        fins.append(h_fin)
    return y, jnp.stack(fins, axis=1)


def _hgrn2_dir(q, k, v, logf, s0):
    bn, ln, nh, dk = q.shape
    dv = v.shape[-1]
    n = ln // HG_CHUNK

    def to_chunks(t):
        return t.reshape(bn, n, HG_CHUNK, nh, t.shape[-1]).transpose(1, 0, 3, 2, 4)

    mask = jnp.tril(jnp.ones((HG_CHUNK, HG_CHUNK), dtype=bool))[:, :, None]

    def step(s, inp):
        qc, kc, vc, lf = inp
        b = jnp.cumsum(lf, axis=2)
        inter = jnp.einsum('bhtd,bhde->bhte', qc * jnp.exp(b), s)
        diff = b[:, :, :, None, :] - b[:, :, None, :, :]
        dec = jnp.where(mask, jnp.exp(jnp.where(mask, diff, 0.0)), 0.0)
        att = jnp.einsum('bhtd,bhtsd,bhsd->bhts', qc, dec, kc)
        intra = jnp.einsum('bhts,bhse->bhte', att, vc)
        bl = b[:, :, -1:, :]
        s_new = jnp.exp(bl[:, :, 0, :])[..., None] * s + jnp.einsum('bhsd,bhse->bhde', kc * jnp.exp(bl - b), vc)
        return s_new, inter + intra

    s_fin, o = lax.scan(step, s0, (to_chunks(q), to_chunks(k), to_chunks(v), to_chunks(logf)))
    o = o.transpose(1, 0, 3, 2, 4).reshape(bn, ln, nh, dv)
    return o, s_fin


def _hgrn2(hq, hf_f, hf_b, hi, lb, s0):
    bn, ln, _ = hq.shape
    q = jax.nn.silu(hq.astype(F32)).reshape(bn, ln, HG_HEADS, HG_DK)
    v = hi.astype(F32).reshape(bn, ln, HG_HEADS, HG_DV)
    y = jnp.zeros((bn, ln, HG_HEADS, HG_DV), F32)
    fins = []
    for d, z in enumerate((hf_f, hf_b)):
        zf = z.astype(F32)
        lbd = lb[d]
        f = lbd + (1.0 - lbd) * jax.nn.sigmoid(zf)
        logf = jnp.log(jnp.maximum(f, F_MIN))
        k = (1.0 - lbd) * jax.nn.sigmoid(-zf)
        logf = logf.reshape(bn, ln, HG_HEADS, HG_DK)
        k = k.reshape(bn, ln, HG_HEADS, HG_DK)
        qd, vd = q, v
        if d == 1:
            qd, k, vd, logf = (jnp.flip(t, 1) for t in (qd, k, vd, logf))
        o, s_fin = _hgrn2_dir(qd, k, vd, logf, s0[:, d].astype(F32))
        if d == 1:
            o = jnp.flip(o, 1)
        y = y + o
        fins.append(s_fin)
    return y.reshape(bn, ln, D_B), jnp.stack(fins, axis=1)


def _rope_grid(x):
    ln = x.shape[1]
    rows = ln // GRID_W
    pos_r = jnp.broadcast_to(jnp.arange(rows, dtype=F32)[:, None], (rows, GRID_W)).reshape(-1)
    pos_c = jnp.broadcast_to(jnp.arange(GRID_W, dtype=F32)[None, :], (rows, GRID_W)).reshape(-1)
    half = x.shape[-1] // 2
    nf = half // 2
    freq = ROPE_BASE ** (-jnp.arange(nf, dtype=F32) / nf)

    def rot(t, pos):
        ang = pos[:, None] * freq[None, :]
        cos = jnp.cos(ang)[None, :, None, :]
        sin = jnp.sin(ang)[None, :, None, :]
        a, b = t[..., :nf], t[..., nf:]
        return jnp.concatenate([a * cos - b * sin, a * sin + b * cos], axis=-1)

    return jnp.concatenate([rot(x[..., :half], pos_r), rot(x[..., half:], pos_c)], axis=-1)


def _ret_dir(q, k, v, lg, s0):
    bn, ln, nh, dk = q.shape
    dv = v.shape[-1]
    c = RET_CHUNK
    n = ln // c
    qc = q.reshape(bn, n, c, nh, dk)
    kc = k.reshape(bn, n, c, nh, dk)
    vc = v.reshape(bn, n, c, nh, dv)
    j = jnp.arange(c, dtype=F32)
    rel = j[:, None] - j[None, :]
    mask = rel >= 0
    dmat = jnp.where(mask[None], jnp.exp(jnp.where(mask, rel, 0.0)[None] * lg[:, None, None]), 0.0)
    sc = jnp.einsum('bnthd,bnshd->bnhts', qc, kc) * dmat[None, None]
    intra = jnp.einsum('bnhts,bnshe->bnthe', sc, vc)
    kv = jnp.einsum('bnshd,hs,bnshe->bnhde', kc, jnp.exp((c - 1.0 - j)[None, :] * lg[:, None]), vc)
    g_c = jnp.exp(c * lg)[:, None, None]

    def step(s, kv_n):
        return g_c * s + kv_n, s

    s_fin, s_before = lax.scan(step, s0, jnp.moveaxis(kv, 1, 0))
    s_before = jnp.moveaxis(s_before, 0, 1)
    inter = jnp.einsum('bnthd,ht,bnhde->bnthe', qc, jnp.exp((j + 1.0)[None, :] * lg[:, None]), s_before)
    return (intra + inter).reshape(bn, ln, nh, dv), s_fin


def _retention(rq, rk, rv, decay_logit, s0, latent):
    bn, ln, _ = rq.shape
    q = rq.astype(F32).reshape(bn, ln, RET_HEADS, RET_DK)
    k = rk.astype(F32).reshape(bn, ln, RET_HEADS, RET_DK)
    v = rv.astype(F32).reshape(bn, ln, RET_HEADS, RET_DV)
    if latent:
        q = _rope_grid(q)
        k = _rope_grid(k)
    q = q * (RET_DK ** -0.5)
    lg_all = jax.nn.log_sigmoid(decay_logit.astype(F32))
    y = jnp.zeros((bn, ln, RET_HEADS, RET_DV), F32)
    fins = []
    for d in range(2):
        qd, kd, vd = q, k, v
        if d == 1:
            qd, kd, vd = (jnp.flip(t, 1) for t in (qd, kd, vd))
        o, s_fin = _ret_dir(qd, kd, vd, lg_all[d], s0[:, d].astype(F32))
        if d == 1:
            o = jnp.flip(o, 1)
        y = y + o
        fins.append(s_fin)
    return y.reshape(bn, ln, D_C), jnp.stack(fins, axis=1)


def _layer(x, cond, s_lru, s_hg, s_ret, p, l, latent):
    m = jax.nn.silu(cond) @ p['w_mod'][l] + p['b_mod'][l]
    shift, scale, gate = jnp.split(m[:, None, :], 3, axis=-1)
    xn = _rmsnorm(x, p['norm_g'][l]) * (1.0 + scale) + shift
    proj = xn @ p['w_in'][l]
    offs = []
    s = 0
    for w in IN_WIDTHS[:-1]:
        s += w
        offs.append(s)
    xa, ga, hq, hf_f, hf_b, hi, hg, rq, rk, rv, rg = jnp.split(proj, offs, axis=-1)
    ya, fin_lru = _rg_lru(xa, p['conv_w'][l], p['conv_b'][l], p['lru_wa'][l], p['lru_ba'][l],
                          p['lru_wx'][l], p['lru_bx'][l], p['lru_lambda'][l], s_lru)
    lbp = jax.nn.softmax(p['hgrn_lb_logits'].astype(F32), axis=0)
    lb = (jnp.cumsum(lbp, axis=0) - lbp[0])[l]
    yb, fin_hg = _hgrn2(hq, hf_f, hf_b, hi, lb, s_hg)
    yb = _head_rmsnorm(yb, p['hgrn_norm_g'][l], HG_HEADS)
    yc, fin_ret = _retention(rq, rk, rv, p['ret_decay_logit'][l], s_ret, latent)
    yc = _head_rmsnorm(yc, p['ret_norm_g'][l], RET_HEADS)
    ycat = jnp.concatenate([ya * jax.nn.silu(ga.astype(F32)), yb * jax.nn.silu(hg.astype(F32)),
                            yc * jax.nn.silu(rg.astype(F32))], axis=-1).astype(x.dtype)
    x = x + gate * (ycat @ p['w_out'][l])
    return x, fin_lru, fin_hg, fin_ret


def setup_inputs(seed: int = 0) -> dict:
    key = jax.random.key(seed)
    ks = jax.random.split(key, 24)
    nrm = jax.random.normal
    u = jax.random.uniform(ks[17], (DEPTH, 2, D_A), F32, 0.9, 0.999)
    a0 = u ** (1.0 / LRU_C)
    ret_base = jnp.log(2.0 ** (5.0 + jnp.arange(RET_HEADS, dtype=F32)) - 1.0)
    return {
        "x_prompt": nrm(ks[0], (BATCH, SEQ, D_MODEL), F32),
        "x_sample": nrm(ks[1], (DEC_BATCH, DEC_SEQ, D_MODEL), F32),
        "state_lru": nrm(ks[2], (DEC_BATCH, DEPTH, 2, D_A), F32),
        "state_hgrn": nrm(ks[3], (DEC_BATCH, DEPTH, 2, HG_HEADS, HG_DK, HG_DV), F32),
        "state_ret": nrm(ks[4], (DEC_BATCH, DEPTH, 2, RET_HEADS, RET_DK, RET_DV), F32),
        "c": nrm(ks[5], (DEC_BATCH, D_MODEL), F32),
        "c_ctx": nrm(ks[6], (D_MODEL,), F32),
        "norm_g": 1.0 + 0.02 * nrm(ks[7], (DEPTH, D_MODEL), F32),
        "w_mod": nrm(ks[8], (DEPTH, D_MODEL, 3 * D_MODEL), F32) * (0.5 * D_MODEL ** -0.5),
        "b_mod": 0.01 * nrm(ks[9], (DEPTH, 3 * D_MODEL), F32),
        "w_in": nrm(ks[10], (DEPTH, D_MODEL, D_IN), F32) * (D_MODEL ** -0.5),
        "conv_w": nrm(ks[11], (DEPTH, CONV_W, D_A), F32) * (CONV_W ** -0.5),
        "conv_b": 0.01 * nrm(ks[12], (DEPTH, D_A), F32),
        "lru_wa": nrm(ks[13], (DEPTH, 2, LRU_BLOCKS, LRU_BW, LRU_BW), F32) * (LRU_BW ** -0.5),
        "lru_ba": 0.01 * nrm(ks[14], (DEPTH, 2, D_A), F32),
        "lru_wx": nrm(ks[15], (DEPTH, 2, LRU_BLOCKS, LRU_BW, LRU_BW), F32) * (LRU_BW ** -0.5),
        "lru_bx": 0.01 * nrm(ks[16], (DEPTH, 2, D_A), F32),
        "lru_lambda": jnp.log(a0) - jnp.log1p(-a0),
        "hgrn_lb_logits": nrm(ks[18], (DEPTH, 2, D_B), F32),
        "hgrn_norm_g": 1.0 + 0.02 * nrm(ks[19], (DEPTH, D_B), F32),
        "ret_decay_logit": ret_base + 0.01 * nrm(ks[20], (DEPTH, 2, RET_HEADS), F32),
        "ret_norm_g": 1.0 + 0.02 * nrm(ks[21], (DEPTH, D_C), F32),
        "w_out": nrm(ks[22], (DEPTH, D_MIX, D_MODEL), F32) * (D_MIX ** -0.5),
        "final_g": 1.0 + 0.02 * nrm(ks[23], (D_MODEL,), F32),
    }


def reference(x_prompt, x_sample, state_lru, state_hgrn, state_ret, c, c_ctx, norm_g, w_mod, b_mod, w_in,
              conv_w, conv_b, lru_wa, lru_ba, lru_wx, lru_bx, lru_lambda, hgrn_lb_logits, hgrn_norm_g,
              ret_decay_logit, ret_norm_g, w_out, final_g):
    p = dict(norm_g=norm_g, w_mod=w_mod, b_mod=b_mod, w_in=w_in, conv_w=conv_w, conv_b=conv_b,
             lru_wa=lru_wa, lru_ba=lru_ba, lru_wx=lru_wx, lru_bx=lru_bx, lru_lambda=lru_lambda,
             hgrn_lb_logits=hgrn_lb_logits, hgrn_norm_g=hgrn_norm_g, ret_decay_logit=ret_decay_logit,
             ret_norm_g=ret_norm_g, w_out=w_out)
    bp = x_prompt.shape[0]
    z_lru = jnp.zeros((bp, 2, D_A), F32)
    z_hg = jnp.zeros((bp, 2, HG_HEADS, HG_DK, HG_DV), F32)
    z_ret = jnp.zeros((bp, 2, RET_HEADS, RET_DK, RET_DV), F32)
    x = x_prompt
    lru_l, hg_l, ret_l = [], [], []
    for l in range(DEPTH):
        x, f_lru, f_hg, f_ret = _layer(x, c_ctx[None, :], z_lru, z_hg, z_ret, p, l, latent=False)
        lru_l.append(f_lru)
        hg_l.append(f_hg)
        ret_l.append(f_ret)
    y_prompt = _rmsnorm(x, final_g)
    new_lru = jnp.stack(lru_l, axis=1)
    new_hgrn = jnp.stack(hg_l, axis=1)
    new_ret = jnp.stack(ret_l, axis=1)
    z = x_sample
    for l in range(DEPTH):
        z, _, _, _ = _layer(z, c, state_lru[:, l], state_hgrn[:, l], state_ret[:, l], p, l, latent=True)
    y_sample = _rmsnorm(z, final_g)
    return (y_prompt, y_sample, new_lru, new_hgrn, new_ret)
```

```python
import functools

import jax
import jax.numpy as jnp
from jax import lax
from jax.experimental import pallas as pl
from jax.experimental.pallas import tpu as pltpu

F32 = jnp.float32
BF16 = jnp.bfloat16

D_MODEL = 1024
DEPTH = 2
GRID_W = 64
D_A = 512
LRU_BLOCKS = 8
LRU_BW = D_A // LRU_BLOCKS
CONV_W = 4
CONV_LEFT = 1
LRU_C = 8.0
D_B = 512
HG_HEADS = 4
HG_DK = D_B // HG_HEADS
HG_CHUNK = 32
F_MIN = 1e-20
D_C = 512
RET_HEADS = 4
RET_DK = D_C // RET_HEADS
ROPE_BASE = 10000.0
D_MIX = D_A + D_B + D_C
D_IN = 2 * D_A + 5 * D_B + 4 * D_C
EPS = 1e-6

COL_A = 0
COL_B = 2 * D_A
COL_C = COL_B + 5 * D_B

TM = 256
HEAD_W = 128
N_HEADS = 4
GATE_HALF = 256
SAFE_LOG_DECAY = 60.0
V7X_VMEM_LIMIT_BYTES = 60000 * 1024
COND_ROWS = 8


def _sigmoid(x):
    return jax.nn.sigmoid(x)


def _silu(x):
    return x * jax.nn.sigmoid(x)


def _dot(a, b):
    return jnp.dot(a, b, preferred_element_type=F32)


def _dot_nt(a, b):
    return lax.dot_general(a, b, (((1,), (1,)), ((), ())), preferred_element_type=F32)


def _dot_tn(a, b):
    return lax.dot_general(a, b, (((0,), (0,)), ((), ())), preferred_element_type=F32)


def _loop(n, body):
    if n == 1:
        body(0)
    else:
        def wrapped(i, carry):
            body(i)
            return carry
        lax.fori_loop(0, n, wrapped, 0)


def _mod_kernel(c_ref, w_ref, b_ref, o_ref):
    c = c_ref[...]
    o_ref[...] = _dot(_silu(c), w_ref[...]) + b_ref[...]


def _modulation(cond, w_mod, b_mod):
    n_col = 3 * D_MODEL
    blk = D_MODEL
    return pl.pallas_call(
        _mod_kernel,
        out_shape=jax.ShapeDtypeStruct((DEPTH, COND_ROWS, n_col), F32),
        grid=(DEPTH, n_col // blk),
        in_specs=[
            pl.BlockSpec((COND_ROWS, D_MODEL), lambda l, j: (0, 0)),
            pl.BlockSpec((None, D_MODEL, blk), lambda l, j: (l, 0, j)),
            pl.BlockSpec((None, 1, blk), lambda l, j: (l, 0, j)),
        ],
        out_specs=pl.BlockSpec((None, COND_ROWS, blk), lambda l, j: (l, 0, j)),
        name="modulation",
    )(cond, w_mod, b_mod.reshape(DEPTH, 1, n_col))


def _layer_kernel(*refs, seq, layer, latent, last):
    n_tiles = seq // TM
    n_chunks = seq // HG_CHUNK
    it = iter(refs)
    x_ref = next(it); mod_ref = next(it); ng_ref = next(it); w_in_ref = next(it)
    cw_ref = next(it); cb_ref = next(it); wblk_ref = next(it); gb_ref = next(it); lam_ref = next(it)
    lbl_ref = next(it); hgn_ref = next(it); rdl_ref = next(it); rtn_ref = next(it)
    w_out_ref = next(it); fg_ref = next(it)
    if latent:
        s_lru_ref = next(it); s_hg_ref = next(it); s_ret_ref = next(it)
        cos_ref = next(it); sina_ref = next(it); sinb_ref = next(it)
    y_ref = next(it)
    if not latent:
        f_lru_ref = next(it); f_hg_ref = next(it); f_ret_ref = next(it)
    xn_s = next(it); ycat_s = next(it); xpad_s = next(it)
    fa_s = next(it); fb_s = next(it); fc_s = next(it); fd_s = next(it)
    hv_s = next(it); hq_s = next(it)
    hqb_s = next(it); hki_s = next(it); hkd_s = next(it); hk_s = next(it)
    ebl_s = next(it); st_s = next(it); minb_s = next(it); flag_s = next(it)
    ds_s = next(it); wtab_s = next(it); gc_s = next(it); kv_s = next(it)

    def rows_of(i):
        return pl.ds(pl.multiple_of(i * TM, TM), TM)

    shift = mod_ref[0:1, :]
    scale = mod_ref[1:2, :]
    gate = mod_ref[2:3, :]

    @pl.when(pl.program_id(0) == 0)
    def _():
        lg_all = -jax.nn.softplus(-rdl_ref[...])
        ti = lax.broadcasted_iota(jnp.int32, (TM, TM), 0)
        si = lax.broadcasted_iota(jnp.int32, (TM, TM), 1)
        rel = (ti - si).astype(F32)
        tcol = lax.broadcasted_iota(jnp.int32, (TM, HEAD_W), 0).astype(F32)
        for h in range(N_HEADS):
            lg_f = lg_all[h:h + 1, :]
            lg_b = lg_all[N_HEADS + h:N_HEADS + h + 1, :]
            d_f = jnp.where(ti >= si, jnp.exp(jnp.where(ti >= si, rel, 0.0) * lg_f), 0.0)
            d_b = jnp.where(si >= ti, jnp.exp(jnp.where(si >= ti, -rel, 0.0) * lg_b), 0.0)
            ds_s[h] = d_f + d_b
            lgf = lg_f[:, :HEAD_W]
            lgb = lg_b[:, :HEAD_W]
            wtab_s[0, h] = jnp.exp((TM - 1.0 - tcol) * lgf)
            wtab_s[1, h] = jnp.exp(tcol * lgb)
            wtab_s[2, h] = jnp.exp((tcol + 1.0) * lgf)
            wtab_s[3, h] = jnp.exp((TM - tcol) * lgb)
            gc_s[0, h] = jnp.broadcast_to(jnp.exp(TM * lgf), (8, HEAD_W))
            gc_s[1, h] = jnp.broadcast_to(jnp.exp(TM * lgb), (8, HEAD_W))

    ng = ng_ref[...]

    def norm_tile(i):
        rows = rows_of(i)
        x = x_ref[rows, :]
        ms = jnp.mean(x * x, axis=-1, keepdims=True)
        xn = (x * lax.rsqrt(ms + EPS)) * ng
        xn = xn * (1.0 + scale) + shift
        xn_s[rows, :] = xn.astype(BF16)

    _loop(n_tiles, norm_tile)

    zero_pad = jnp.zeros((8, D_A), F32)
    xpad_s[pl.ds(0, 8), :] = zero_pad
    xpad_s[pl.ds(seq + 8, 8), :] = zero_pad

    def lru_proj_tile(i):
        rows = rows_of(i)
        pa = _dot(xn_s[rows, :], w_in_ref[:, COL_A:COL_A + 2 * D_A])
        xpad_s[pl.ds(pl.multiple_of(i * TM + 8, 8), TM), :] = pa[:, :D_A]
        ycat_s[rows, 0:D_A] = _silu(pa[:, D_A:]).astype(BF16)

    _loop(n_tiles, lru_proj_tile)

    cw = cw_ref[...]
    cb = cb_ref[...]
    nsp = jax.nn.softplus(-lam_ref[...])
    a_refs = (fa_s, fc_s)
    b_refs = (fb_s, fd_s)

    def lru_gate_tile(i):
        rows = rows_of(i)
        win = xpad_s[pl.ds(pl.multiple_of(i * TM, TM), TM + 16), :]
        wn = TM + 16
        u = cb
        for k in range(CONV_W):
            off = k - CONV_LEFT
            sh = win if off == 0 else pltpu.roll(win, (-off) % wn, 0)
            u = u + sh[8:8 + TM, :] * cw[k:k + 1, :]
        for hf in range(D_A // GATE_HALF):
            cols = slice(hf * GATE_HALF, (hf + 1) * GATE_HALF)
            uh = u[:, cols]
            g4 = _dot(uh.astype(BF16), wblk_ref[hf])
            for d in range(2):
                c0 = 2 * d * GATE_HALF
                r = _sigmoid(g4[:, c0:c0 + GATE_HALF] + gb_ref[2 * d:2 * d + 1, cols])
                ig = _sigmoid(g4[:, c0 + GATE_HALF:c0 + 2 * GATE_HALF] + gb_ref[2 * d + 1:2 * d + 2, cols])
                log_a = (-LRU_C) * r * nsp[d:d + 1, cols]
                th = jnp.tanh(log_a)
                bcoef = jnp.sqrt((-2.0 * th) / (1.0 - th))
                a_refs[d][rows, cols] = jnp.exp(log_a)
                b_refs[d][rows, cols] = bcoef * (ig * uh)

    _loop(n_tiles, lru_gate_tile)

    if latent:
        h0f = s_lru_ref[0:1, :]
        h0b = s_lru_ref[1:2, :]
    else:
        h0f = jnp.zeros((1, D_A), F32)
        h0b = h0f

    def lru_step(t, carry):
        hf, hb = carry
        rf = pl.ds(t, 1)
        hf = fa_s[rf, :] * hf + fb_s[rf, :]
        fa_s[rf, :] = hf
        rb = pl.ds(seq - 1 - t, 1)
        hb = fc_s[rb, :] * hb + fd_s[rb, :]
        fc_s[rb, :] = hb
        return hf, hb

    hf_fin, hb_fin = lax.fori_loop(0, seq, lru_step, (h0f, h0b), unroll=8)
    if not latent:
        f_lru_ref[0:1, :] = hf_fin
        f_lru_ref[1:2, :] = hb_fin

    def lru_out_tile(i):
        rows = rows_of(i)
        ya = (fa_s[rows, :] + fc_s[rows, :]) * ycat_s[rows, 0:D_A].astype(F32)
        ycat_s[rows, 0:D_A] = ya.astype(BF16)

    _loop(n_tiles, lru_out_tile)

    lbl = lbl_ref[...]
    lb_rows = []
    for d in range(2):
        ls = [lbl[k * 2 + d:k * 2 + d + 1, :] for k in range(DEPTH)]
        mx = functools.reduce(jnp.maximum, ls)
        es = [jnp.exp(v - mx) for v in ls]
        den = functools.reduce(lambda p, q: p + q, es)
        acc = jnp.zeros_like(mx)
        for k in range(1, layer + 1):
            acc = acc + es[k] / den
        lb_rows.append(acc)

    ri = lax.broadcasted_iota(jnp.int32, (TM, TM), 0)
    ci = lax.broadcasted_iota(jnp.int32, (TM, TM), 1)
    same_chunk = (ri // HG_CHUNK) == (ci // HG_CHUNK)
    cum_mats = (jnp.where(same_chunk & (ci <= ri), 1.0, 0.0).astype(BF16),
                jnp.where(same_chunk & (ci >= ri), 1.0, 0.0).astype(BF16))
    cpt = TM // HG_CHUNK

    def hg_proj(d):
        minb_s[...] = jnp.zeros_like(minb_s)

        def tile(i):
            rows = rows_of(i)
            xn = xn_s[rows, :]
            if d == 0:
                p = _dot(xn, w_in_ref[:, COL_B:COL_B + 2 * D_B])
                pv = _dot(xn, w_in_ref[:, COL_B + 3 * D_B:COL_B + 5 * D_B])
                q = _silu(p[:, 0:D_B])
                z = p[:, D_B:2 * D_B]
                hv_s[rows, :] = pv[:, 0:D_B].astype(BF16)
                ycat_s[rows, D_A:D_A + D_B] = _silu(pv[:, D_B:2 * D_B]).astype(BF16)
                hq_s[rows, :] = q.astype(BF16)
            else:
                z = _dot(xn, w_in_ref[:, COL_B + 2 * D_B:COL_B + 3 * D_B])
                q = hq_s[rows, :].astype(F32)
            lb = lb_rows[d]
            f = lb + (1.0 - lb) * _sigmoid(z)
            logf = jnp.log(jnp.maximum(f, F_MIN))
            k = (1.0 - lb) * _sigmoid(-z)
            hi = logf.astype(BF16)
            lo = (logf - hi.astype(F32)).astype(BF16)
            b = _dot(cum_mats[d], hi) + _dot(cum_mats[d], lo)
            b3 = b.reshape(cpt, HG_CHUNK, D_B)
            edge = HG_CHUNK - 1 if d == 0 else 0
            bl3 = b3[:, edge:edge + 1, :]
            bl = jnp.broadcast_to(bl3, b3.shape).reshape(TM, D_B)
            hqb_s[rows, :] = (q * jnp.exp(b)).astype(BF16)
            hki_s[rows, :] = (k * jnp.exp(jnp.minimum(-b, 80.0))).astype(BF16)
            hkd_s[rows, :] = (k * jnp.exp(bl - b)).astype(BF16)
            hk_s[rows, :] = k.astype(BF16)
            fa_s[rows, :] = b
            bl2 = bl3.reshape(cpt, D_B)
            ebl_s[pl.ds(pl.multiple_of(i * cpt, cpt), cpt), :] = jnp.exp(bl2)
            minb_s[...] = jnp.minimum(minb_s[...], jnp.min(bl2, axis=0, keepdims=True))

        _loop(n_tiles, tile)
        flag_s[0] = (jnp.min(minb_s[...]) >= -SAFE_LOG_DECAY).astype(jnp.int32)

    for d in range(2):
        for h in range(N_HEADS):
            if latent:
                st_s[d * N_HEADS + h] = s_hg_ref[d, h].T
            else:
                st_s[d * N_HEADS + h] = jnp.zeros((HEAD_W, HEAD_W), F32)

    ri32 = lax.broadcasted_iota(jnp.int32, (HG_CHUNK, HG_CHUNK), 0)
    ci32 = lax.broadcasted_iota(jnp.int32, (HG_CHUNK, HG_CHUNK), 1)
    rrow = lax.broadcasted_iota(jnp.int32, (HG_CHUNK, HEAD_W), 0)

    def hg_chunk(d, first_dir):
        def body(j):
            c = j if d == 0 else n_chunks - 1 - j
            r0 = pl.multiple_of(c * HG_CHUNK, HG_CHUNK)
            rws = pl.ds(r0, HG_CHUNK)
            ebl = ebl_s[pl.ds(c, 1), :]
            for h in range(N_HEADS):
                hs = slice(h * HEAD_W, (h + 1) * HEAD_W)
                qb = hqb_s[rws, hs]
                vb = hv_s[rws, hs]
                st = st_s[d * N_HEADS + h]
                inter = _dot_nt(qb, st.astype(BF16))

                def fast():
                    att = _dot_nt(qb, hki_s[rws, hs])
                    keep = (ri32 >= ci32) if d == 0 else (ri32 <= ci32)
                    return _dot(jnp.where(keep, att, 0.0).astype(BF16), vb)

                def slow():
                    qf = hq_s[rws, hs].astype(F32)
                    kf = hk_s[rws, hs].astype(F32)
                    vf = vb.astype(F32)
                    bf = fa_s[rws, hs]

                    def src(s, acc):
                        pick = rrow == s
                        ks = jnp.sum(jnp.where(pick, kf, 0.0), axis=0, keepdims=True)
                        vs = jnp.sum(jnp.where(pick, vf, 0.0), axis=0, keepdims=True)
                        bs = jnp.sum(jnp.where(pick, bf, 0.0), axis=0, keepdims=True)
                        w = jnp.sum(qf * ks * jnp.exp(jnp.minimum(bf - bs, 0.0)), axis=-1, keepdims=True)
                        keep = (rrow[:, 0:1] >= s) if d == 0 else (rrow[:, 0:1] <= s)
                        return acc + jnp.where(keep, w, 0.0) * vs

                    return lax.fori_loop(0, HG_CHUNK, src, jnp.zeros((HG_CHUNK, HEAD_W), F32))

                intra = lax.cond(flag_s[0] == 1, fast, slow)
                o = inter + intra
                if first_dir:
                    fc_s[rws, hs] = o
                else:
                    fc_s[rws, hs] = fc_s[rws, hs] + o
                st_s[d * N_HEADS + h] = st * ebl[:, hs] + _dot_tn(vb, hkd_s[rws, hs])
        _loop(n_chunks, body)

    hg_proj(0)
    hg_chunk(0, True)
    hg_proj(1)
    hg_chunk(1, False)
    if not latent:
        for d in range(2):
            for h in range(N_HEADS):
                f_hg_ref[d, h] = st_s[d * N_HEADS + h].T

    def head_norm_gate(y, gain_ref, col0, rows):
        for h in range(N_HEADS):
            hs = slice(h * HEAD_W, (h + 1) * HEAD_W)
            yh = y[:, hs]
            ms = jnp.mean(yh * yh, axis=-1, keepdims=True)
            yn = yh * lax.rsqrt(ms + EPS) * gain_ref[:, hs]
            cs = slice(col0 + h * HEAD_W, col0 + (h + 1) * HEAD_W)
            ycat_s[rows, cs] = (yn * ycat_s[rows, cs].astype(F32)).astype(BF16)

    def hg_out_tile(i):
        rows = rows_of(i)
        head_norm_gate(fc_s[rows, :], hgn_ref, D_A, rows)

    _loop(n_tiles, hg_out_tile)

    def ret_tile(i):
        rows = rows_of(i)
        p = _dot(xn_s[rows, :], w_in_ref[:, COL_C:COL_C + 4 * D_C])
        ycat_s[rows, D_A + D_B:D_MIX] = _silu(p[:, 3 * D_C:4 * D_C]).astype(BF16)
        for h in range(N_HEADS):
            hs = slice(h * HEAD_W, (h + 1) * HEAD_W)
            qh = p[:, hs]
            kh = p[:, D_C + h * HEAD_W:D_C + (h + 1) * HEAD_W]
            vh = p[:, 2 * D_C + h * HEAD_W:2 * D_C + (h + 1) * HEAD_W].astype(BF16)
            if latent:
                cos = cos_ref[rows, :]
                sa = sina_ref[rows, :]
                sb = sinb_ref[rows, :]
                qh = qh * cos + pltpu.roll(qh, HEAD_W - 32, 1) * sa + pltpu.roll(qh, 32, 1) * sb
                kh = kh * cos + pltpu.roll(kh, HEAD_W - 32, 1) * sa + pltpu.roll(kh, 32, 1) * sb
            qh = qh * (RET_DK ** -0.5)
            qb = qh.astype(BF16)
            hq_s[rows, hs] = qb
            sc = _dot_nt(qb, kh.astype(BF16)) * ds_s[h]
            fa_s[rows, hs] = _dot(sc.astype(BF16), vh)
            for d in range(2):
                kw = (kh * wtab_s[d, h]).astype(BF16)
                kv_s[(i * 2 + d) * N_HEADS + h] = _dot_tn(kw, vh)

    _loop(n_tiles, ret_tile)

    carried = latent or n_tiles > 1
    for d in range(2):
        for h in range(N_HEADS):
            hs = slice(h * HEAD_W, (h + 1) * HEAD_W)
            if latent:
                state = s_ret_ref[d, h]
            else:
                state = jnp.zeros((HEAD_W, HEAD_W), F32)
            g_c = gc_s[d, h][0:1, :]
            order = range(n_tiles) if d == 0 else range(n_tiles - 1, -1, -1)
            for n in order:
                rows = pl.ds(n * TM, TM)
                if carried:
                    qw = (hq_s[rows, hs].astype(F32) * wtab_s[2 + d, h]).astype(BF16)
                    fa_s[rows, hs] = fa_s[rows, hs] + _dot(qw, state.astype(BF16))
                state = state * g_c + kv_s[(n * 2 + d) * N_HEADS + h]
            if not latent:
                f_ret_ref[d, h] = state

    def ret_out_tile(i):
        rows = rows_of(i)
        head_norm_gate(fa_s[rows, :], rtn_ref, D_A + D_B, rows)

    _loop(n_tiles, ret_out_tile)

    fg = fg_ref[...]

    def out_tile(i):
        rows = rows_of(i)
        delta = _dot(ycat_s[rows, :], w_out_ref[...])
        out = x_ref[rows, :] + gate * delta
        if last:
            ms = jnp.mean(out * out, axis=-1, keepdims=True)
            out = out * lax.rsqrt(ms + EPS) * fg
        y_ref[rows, :] = out

    _loop(n_tiles, out_tile)


def _const_spec(shape):
    zeros = (0,) * len(shape)
    return pl.BlockSpec(shape, lambda b: zeros, pipeline_mode=pl.Buffered(1))


def _layer_call(x, mod, layer, latent, last, prm, states, rope):
    bsz, seq, _ = x.shape
    n_tiles = seq // TM
    n_chunks = seq // HG_CHUNK
    per_seq = lambda *tail: pl.BlockSpec((None,) + tail, lambda b: (b,) + (0,) * len(tail))
    mod_spec = (per_seq(3, D_MODEL) if latent
                else pl.BlockSpec((None, 3, D_MODEL), lambda b: (0, 0, 0)))
    operands = [x, mod, prm["norm_g"], prm["w_in"], prm["conv_w"], prm["conv_b"], prm["wblk"], prm["gbias"],
                prm["lam"], prm["lb_logits"], prm["hgrn_norm_g"], prm["rdl"], prm["ret_norm_g"], prm["w_out"],
                prm["final_g"]]
    act_mode = dict(pipeline_mode=pl.Buffered(1)) if latent else {}
    act_spec = pl.BlockSpec((None, seq, D_MODEL), lambda b: (b, 0, 0), **act_mode)
    in_specs = [act_spec, mod_spec] + [_const_spec(o.shape) for o in operands[2:]]
    if latent:
        operands += list(states) + list(rope)
        in_specs += [per_seq(2, D_A), per_seq(2, N_HEADS, HEAD_W, HEAD_W), per_seq(2, N_HEADS, HEAD_W, HEAD_W)]
        in_specs += [_const_spec(r.shape) for r in rope]
    out_shape = [jax.ShapeDtypeStruct((bsz, seq, D_MODEL), F32)]
    out_specs = [act_spec]
    if not latent:
        out_shape += [jax.ShapeDtypeStruct((bsz, 2, D_A), F32),
                      jax.ShapeDtypeStruct((bsz, 2, N_HEADS, HEAD_W, HEAD_W), F32),
                      jax.ShapeDtypeStruct((bsz, 2, N_HEADS, HEAD_W, HEAD_W), F32)]
        out_specs += [per_seq(2, D_A), per_seq(2, N_HEADS, HEAD_W, HEAD_W), per_seq(2, N_HEADS, HEAD_W, HEAD_W)]
    scratch = [
        pltpu.VMEM((seq, D_MODEL), BF16),
        pltpu.VMEM((seq, D_MIX), BF16),
        pltpu.VMEM((seq + 16, D_A), F32),
        pltpu.VMEM((seq, D_A), F32), pltpu.VMEM((seq, D_A), F32),
        pltpu.VMEM((seq, D_A), F32), pltpu.VMEM((seq, D_A), F32),
        pltpu.VMEM((seq, D_B), BF16), pltpu.VMEM((seq, D_B), BF16),
        pltpu.VMEM((seq, D_B), BF16), pltpu.VMEM((seq, D_B), BF16),
        pltpu.VMEM((seq, D_B), BF16), pltpu.VMEM((seq, D_B), BF16),
        pltpu.VMEM((n_chunks, D_B), F32),
        pltpu.VMEM((2 * N_HEADS, HEAD_W, HEAD_W), F32),
        pltpu.VMEM((1, D_B), F32),
        pltpu.SMEM((1,), jnp.int32),
        pltpu.VMEM((N_HEADS, TM, TM), F32),
        pltpu.VMEM((4, N_HEADS, TM, HEAD_W), F32),
        pltpu.VMEM((2, N_HEADS, 8, HEAD_W), F32),
        pltpu.VMEM((n_tiles * 2 * N_HEADS, HEAD_W, HEAD_W), F32),
    ]
    body = functools.partial(_layer_kernel, seq=seq, layer=layer, latent=latent, last=last)
    return pl.pallas_call(
        body,
        out_shape=out_shape,
        grid=(bsz,),
        in_specs=in_specs,
        out_specs=out_specs,
        scratch_shapes=scratch,
        compiler_params=pltpu.CompilerParams(dimension_semantics=("arbitrary",),
                                             vmem_limit_bytes=V7X_VMEM_LIMIT_BYTES),
        name=("latent" if latent else "context") + f"_layer{layer}",
    )(*operands)


def _rope_tables(seq):
    half = RET_DK // 2
    nf = half // 2
    t = jnp.arange(seq)
    pos_r = (t // GRID_W).astype(F32)
    pos_c = (t % GRID_W).astype(F32)
    freq = ROPE_BASE ** (-jnp.arange(nf, dtype=F32) / nf)
    ang_r = pos_r[:, None] * freq[None, :]
    ang_c = pos_c[:, None] * freq[None, :]
    zero = jnp.zeros((seq, nf), F32)
    cos = jnp.concatenate([jnp.cos(ang_r)] * 2 + [jnp.cos(ang_c)] * 2, axis=-1)
    sin_a = jnp.concatenate([-jnp.sin(ang_r), zero, -jnp.sin(ang_c), zero], axis=-1)
    sin_b = jnp.concatenate([zero, jnp.sin(ang_r), zero, jnp.sin(ang_c)], axis=-1)
    return cos, sin_a, sin_b


def _block_diag_gates(wa, wx):
    per_half = GATE_HALF // LRU_BW
    eye = jnp.eye(per_half, dtype=F32)
    halves = []
    for hf in range(D_A // GATE_HALF):
        cols = []
        for d in range(2):
            for w in (wa, wx):
                blk = w[d, hf * per_half:(hf + 1) * per_half]
                dense = jnp.einsum('pq,pij->piqj', eye, blk).reshape(GATE_HALF, GATE_HALF)
                cols.append(dense)
        halves.append(jnp.concatenate(cols, axis=-1))
    return jnp.stack(halves).astype(BF16)


def kernel(x_prompt, x_sample, state_lru, state_hgrn, state_ret, c, c_ctx, norm_g, w_mod, b_mod, w_in, conv_w,
           conv_b, lru_wa, lru_ba, lru_wx, lru_bx, lru_lambda, hgrn_lb_logits, hgrn_norm_g, ret_decay_logit,
           ret_norm_g, w_out, final_g):
    dec_b = x_sample.shape[0]
    cond = jnp.zeros((COND_ROWS, D_MODEL), F32).at[0].set(c_ctx).at[1:1 + dec_b].set(c)
    mods = _modulation(cond, w_mod, b_mod).reshape(DEPTH, COND_ROWS, 3, D_MODEL)

    prms = []
    for l in range(DEPTH):
        prms.append(dict(
            norm_g=norm_g[l][None, :],
            w_in=w_in[l].astype(BF16),
            conv_w=conv_w[l],
            conv_b=conv_b[l][None, :],
            wblk=_block_diag_gates(lru_wa[l], lru_wx[l]),
            gbias=jnp.stack([lru_ba[l, 0], lru_bx[l, 0], lru_ba[l, 1], lru_bx[l, 1]]),
            lam=lru_lambda[l],
            lb_logits=hgrn_lb_logits.reshape(DEPTH * 2, D_B),
            hgrn_norm_g=hgrn_norm_g[l][None, :],
            rdl=jnp.broadcast_to(ret_decay_logit[l].reshape(2 * RET_HEADS, 1), (2 * RET_HEADS, TM)),
            ret_norm_g=ret_norm_g[l][None, :],
            w_out=w_out[l].astype(BF16),
            final_g=final_g[None, :],
        ))

    x = x_prompt
    lru_l, hg_l, ret_l = [], [], []
    for l in range(DEPTH):
        x, f_lru, f_hg, f_ret = _layer_call(x, mods[l, 0:1], l, False, l == DEPTH - 1, prms[l], None, None)
        lru_l.append(f_lru)
        hg_l.append(f_hg)
        ret_l.append(f_ret)
    y_prompt = x

    rope = _rope_tables(x_sample.shape[1])
    z = x_sample
    for l in range(DEPTH):
        states = (state_lru[:, l], state_hgrn[:, l], state_ret[:, l])
        (z,) = _layer_call(z, mods[l, 1:1 + dec_b], l, True, l == DEPTH - 1, prms[l], states, rope)
    y_sample = z

    return (y_prompt, y_sample, jnp.stack(lru_l, axis=1), jnp.stack(hg_l, axis=1), jnp.stack(ret_l, axis=1))
```

```python
import functools

import jax
import jax.numpy as jnp
from jax import lax
from jax.experimental import pallas as pl
from jax.experimental.pallas import tpu as pltpu

F32 = jnp.float32
BF16 = jnp.bfloat16

D_MODEL = 1024
DEPTH = 2
GRID_W = 64
D_A = 512
LRU_BLOCKS = 8
LRU_BW = D_A // LRU_BLOCKS
CONV_W = 4
CONV_LEFT = 1
LRU_C = 8.0
D_B = 512
HG_HEADS = 4
HG_DK = D_B // HG_HEADS
HG_CHUNK = 32
F_MIN = 1e-20
D_C = 512
RET_HEADS = 4
RET_DK = D_C // RET_HEADS
ROPE_BASE = 10000.0
D_MIX = D_A + D_B + D_C
D_IN = 2 * D_A + 5 * D_B + 4 * D_C
EPS = 1e-6

COL_A = 0
COL_B = 2 * D_A
COL_C = COL_B + 5 * D_B

TM = 256
HEAD_W = 128
N_HEADS = 4
GATE_HALF = 256
SAFE_LOG_DECAY = 60.0
V7X_VMEM_LIMIT_BYTES = 60000 * 1024
COND_ROWS = 8


def _sigmoid(x):
    return jax.nn.sigmoid(x)


def _silu(x):
    return x * jax.nn.sigmoid(x)


def _dot(a, b):
    return jnp.dot(a, b, preferred_element_type=F32)


def _dot_nt(a, b):
    return lax.dot_general(a, b, (((1,), (1,)), ((), ())), preferred_element_type=F32)


def _dot_tn(a, b):
    return lax.dot_general(a, b, (((0,), (0,)), ((), ())), preferred_element_type=F32)


def _loop(n, body, unroll=1):
    if n == 1:
        body(0)
    else:
        def wrapped(i, carry):
            body(i)
            return carry
        lax.fori_loop(0, n, wrapped, 0, unroll=unroll)


def _mod_kernel(c_ref, w_ref, b_ref, o_ref):
    c = c_ref[...]
    o_ref[...] = _dot(_silu(c), w_ref[...]) + b_ref[...]


def _modulation(cond, w_mod, b_mod):
    n_col = 3 * D_MODEL
    blk = D_MODEL
    return pl.pallas_call(
        _mod_kernel,
        out_shape=jax.ShapeDtypeStruct((DEPTH, COND_ROWS, n_col), F32),
        grid=(DEPTH, n_col // blk),
        in_specs=[
            pl.BlockSpec((COND_ROWS, D_MODEL), lambda l, j: (0, 0)),
            pl.BlockSpec((None, D_MODEL, blk), lambda l, j: (l, 0, j)),
            pl.BlockSpec((None, 1, blk), lambda l, j: (l, 0, j)),
        ],
        out_specs=pl.BlockSpec((None, COND_ROWS, blk), lambda l, j: (l, 0, j)),
        name="modulation",
    )(cond, w_mod, b_mod.reshape(DEPTH, 1, n_col))


def _layer_kernel(*refs, seq, layer, latent, last):
    n_tiles = seq // TM
    n_chunks = seq // HG_CHUNK
    it = iter(refs)
    x_ref = next(it); mod_ref = next(it); ng_ref = next(it); w_in_ref = next(it)
    cw_ref = next(it); cb_ref = next(it); wblk_ref = next(it); gb_ref = next(it); lam_ref = next(it)
    lbl_ref = next(it); hgn_ref = next(it); rdl_ref = next(it); rtn_ref = next(it)
    w_out_ref = next(it); fg_ref = next(it)
    if latent:
        s_lru_ref = next(it); s_hg_ref = next(it); s_ret_ref = next(it)
        cos_ref = next(it); sina_ref = next(it); sinb_ref = next(it)
    y_ref = next(it)
    if not latent:
        f_lru_ref = next(it); f_hg_ref = next(it); f_ret_ref = next(it)
    xn_s = next(it); ycat_s = next(it); xpad_s = next(it)
    fa_s = next(it); fb_s = next(it); fc_s = next(it); fd_s = next(it)
    hv_s = next(it); hq_s = next(it); hqb_s = next(it); hkd_s = next(it)
    ebl_s = next(it); st_s = next(it)
    ds_s = next(it); wtab_s = next(it); gc_s = next(it); kv_s = next(it)

    def rows_of(i):
        return pl.ds(pl.multiple_of(i * TM, TM), TM)

    shift = mod_ref[0:1, :]
    scale = mod_ref[1:2, :]
    gate = mod_ref[2:3, :]

    @pl.when(pl.program_id(0) == 0)
    def _():
        lg_all = -jax.nn.softplus(-rdl_ref[...])
        ti = lax.broadcasted_iota(jnp.int32, (TM, TM), 0)
        si = lax.broadcasted_iota(jnp.int32, (TM, TM), 1)
        rel = (ti - si).astype(F32)
        tcol = lax.broadcasted_iota(jnp.int32, (TM, HEAD_W), 0).astype(F32)
        for h in range(N_HEADS):
            lg_f = lg_all[h:h + 1, :]
            lg_b = lg_all[N_HEADS + h:N_HEADS + h + 1, :]
            d_f = jnp.where(ti >= si, jnp.exp(jnp.where(ti >= si, rel, 0.0) * lg_f), 0.0)
            d_b = jnp.where(si >= ti, jnp.exp(jnp.where(si >= ti, -rel, 0.0) * lg_b), 0.0)
            ds_s[h] = d_f + d_b
            lgf = lg_f[:, :HEAD_W]
            lgb = lg_b[:, :HEAD_W]
            wtab_s[0, h] = jnp.exp((TM - 1.0 - tcol) * lgf)
            wtab_s[1, h] = jnp.exp(tcol * lgb)
            wtab_s[2, h] = jnp.exp((tcol + 1.0) * lgf)
            wtab_s[3, h] = jnp.exp((TM - tcol) * lgb)
            gc_s[0, h] = jnp.broadcast_to(jnp.exp(TM * lgf), (8, HEAD_W))
            gc_s[1, h] = jnp.broadcast_to(jnp.exp(TM * lgb), (8, HEAD_W))

    ng = ng_ref[...]

    def norm_tile(i):
        rows = rows_of(i)
        x = x_ref[rows, :]
        ms = jnp.mean(x * x, axis=-1, keepdims=True)
        xn = (x * lax.rsqrt(ms + EPS)) * ng
        xn = xn * (1.0 + scale) + shift
        xn_s[rows, :] = xn.astype(BF16)

    _loop(n_tiles, norm_tile)

    zero_pad = jnp.zeros((8, D_A), F32)
    xpad_s[pl.ds(0, 8), :] = zero_pad
    xpad_s[pl.ds(seq + 8, 8), :] = zero_pad

    def lru_proj_tile(i):
        rows = rows_of(i)
        pa = _dot(xn_s[rows, :], w_in_ref[:, COL_A:COL_A + 2 * D_A])
        xpad_s[pl.ds(pl.multiple_of(i * TM + 8, 8), TM), :] = pa[:, :D_A]
        ycat_s[rows, 0:D_A] = _silu(pa[:, D_A:]).astype(BF16)

    _loop(n_tiles, lru_proj_tile)

    cw = cw_ref[...]
    cb = cb_ref[...]
    nsp = jax.nn.softplus(-lam_ref[...])
    a_refs = (fa_s, fc_s)
    b_refs = (fb_s, fd_s)

    def lru_gate_tile(i):
        rows = rows_of(i)
        win = xpad_s[pl.ds(pl.multiple_of(i * TM, TM), TM + 16), :]
        wn = TM + 16
        u = cb
        for k in range(CONV_W):
            off = k - CONV_LEFT
            sh = win if off == 0 else pltpu.roll(win, (-off) % wn, 0)
            u = u + sh[8:8 + TM, :] * cw[k:k + 1, :]
        for hf in range(D_A // GATE_HALF):
            cols = slice(hf * GATE_HALF, (hf + 1) * GATE_HALF)
            uh = u[:, cols]
            g4 = _dot(uh.astype(BF16), wblk_ref[hf])
            for d in range(2):
                c0 = 2 * d * GATE_HALF
                r = _sigmoid(g4[:, c0:c0 + GATE_HALF] + gb_ref[2 * d:2 * d + 1, cols])
                ig = _sigmoid(g4[:, c0 + GATE_HALF:c0 + 2 * GATE_HALF] + gb_ref[2 * d + 1:2 * d + 2, cols])
                log_a = (-LRU_C) * r * nsp[d:d + 1, cols]
                th = jnp.tanh(log_a)
                bcoef = jnp.sqrt((-2.0 * th) / (1.0 - th))
                a_refs[d][rows, cols] = jnp.exp(log_a)
                b_refs[d][rows, cols] = bcoef * (ig * uh)

    _loop(n_tiles, lru_gate_tile)

    if latent:
        h0f = s_lru_ref[0:1, :]
        h0b = s_lru_ref[1:2, :]
    else:
        h0f = jnp.zeros((1, D_A), F32)
        h0b = h0f

    def lru_step(t, carry):
        hf, hb = carry
        rf = pl.ds(t, 1)
        hf = fa_s[rf, :] * hf + fb_s[rf, :]
        fa_s[rf, :] = hf
        rb = pl.ds(seq - 1 - t, 1)
        hb = fc_s[rb, :] * hb + fd_s[rb, :]
        fc_s[rb, :] = hb
        return hf, hb

    hf_fin, hb_fin = lax.fori_loop(0, seq, lru_step, (h0f, h0b), unroll=8)
    if not latent:
        f_lru_ref[0:1, :] = hf_fin
        f_lru_ref[1:2, :] = hb_fin

    def lru_out_tile(i):
        rows = rows_of(i)
        ya = (fa_s[rows, :] + fc_s[rows, :]) * ycat_s[rows, 0:D_A].astype(F32)
        ycat_s[rows, 0:D_A] = ya.astype(BF16)

    _loop(n_tiles, lru_out_tile)

    lbl = lbl_ref[...]
    lb_rows = []
    for d in range(2):
        ls = [lbl[k * 2 + d:k * 2 + d + 1, :] for k in range(DEPTH)]
        mx = functools.reduce(jnp.maximum, ls)
        es = [jnp.exp(v - mx) for v in ls]
        den = functools.reduce(lambda p, q: p + q, es)
        acc = jnp.zeros_like(mx)
        for k in range(1, layer + 1):
            acc = acc + es[k] / den
        lb_rows.append(acc)

    ri = lax.broadcasted_iota(jnp.int32, (TM, TM), 0)
    ci = lax.broadcasted_iota(jnp.int32, (TM, TM), 1)
    same_chunk = (ri // HG_CHUNK) == (ci // HG_CHUNK)
    keeps = (same_chunk & (ci <= ri), same_chunk & (ci >= ri))
    cum_mats = tuple(jnp.where(kp, 1.0, 0.0).astype(BF16) for kp in keeps)
    cpt = TM // HG_CHUNK
    rrow = lax.broadcasted_iota(jnp.int32, (HG_CHUNK, HEAD_W), 0)

    def hg_tile(i):
        rows = rows_of(i)
        p = _dot(xn_s[rows, :], w_in_ref[:, COL_B:COL_B + 5 * D_B])
        q = _silu(p[:, 0:D_B])
        v = p[:, 3 * D_B:4 * D_B].astype(BF16)
        hv_s[rows, :] = v
        ycat_s[rows, D_A:D_A + D_B] = _silu(p[:, 4 * D_B:5 * D_B]).astype(BF16)
        for d in range(2):
            z = p[:, (1 + d) * D_B:(2 + d) * D_B]
            lb = lb_rows[d]
            f = lb + (1.0 - lb) * _sigmoid(z)
            logf = jnp.log(jnp.maximum(f, F_MIN))
            k = (1.0 - lb) * _sigmoid(-z)
            hi = logf.astype(BF16)
            lo = (logf - hi.astype(F32)).astype(BF16)
            b = _dot(cum_mats[d], hi) + _dot(cum_mats[d], lo)
            b3 = b.reshape(cpt, HG_CHUNK, D_B)
            edge = HG_CHUNK - 1 if d == 0 else 0
            bl3 = b3[:, edge:edge + 1, :]
            bl = jnp.broadcast_to(bl3, b3.shape).reshape(TM, D_B)
            qb = (q * jnp.exp(b)).astype(BF16)
            hqb_s[d, rows, :] = qb
            hkd_s[d, rows, :] = (k * jnp.exp(bl - b)).astype(BF16)
            bl2 = bl3.reshape(cpt, D_B)
            ebl_s[pl.ds(pl.multiple_of(d * n_chunks + i * cpt, cpt), cpt), :] = jnp.exp(bl2)

            def fast():
                ki = (k * jnp.exp(-b)).astype(BF16)
                for h in range(N_HEADS):
                    hs = slice(h * HEAD_W, (h + 1) * HEAD_W)
                    att = _dot_nt(qb[:, hs], ki[:, hs])
                    o = _dot(jnp.where(keeps[d], att, 0.0).astype(BF16), v[:, hs])
                    fc_s[rows, hs] = o if d == 0 else fc_s[rows, hs] + o

            def slow():
                fa_s[rows, :] = b
                fb_s[rows, :] = q
                fd_s[rows, :] = k

                def chunk(c):
                    rws = pl.ds(pl.multiple_of(i * TM + c * HG_CHUNK, HG_CHUNK), HG_CHUNK)
                    for h in range(N_HEADS):
                        hs = slice(h * HEAD_W, (h + 1) * HEAD_W)
                        bf = fa_s[rws, hs]
                        qf = fb_s[rws, hs]
                        kf = fd_s[rws, hs]
                        vf = hv_s[rws, hs].astype(F32)

                        def src(s, acc):
                            pick = rrow == s
                            ks = jnp.sum(jnp.where(pick, kf, 0.0), axis=0, keepdims=True)
                            vs = jnp.sum(jnp.where(pick, vf, 0.0), axis=0, keepdims=True)
                            bs = jnp.sum(jnp.where(pick, bf, 0.0), axis=0, keepdims=True)
                            w = jnp.sum(qf * ks * jnp.exp(jnp.minimum(bf - bs, 0.0)), axis=-1, keepdims=True)
                            keep = (rrow[:, 0:1] >= s) if d == 0 else (rrow[:, 0:1] <= s)
                            return acc + jnp.where(keep, w, 0.0) * vs

                        o = lax.fori_loop(0, HG_CHUNK, src, jnp.zeros((HG_CHUNK, HEAD_W), F32))
                        fc_s[rws, hs] = o if d == 0 else fc_s[rws, hs] + o

                _loop(cpt, chunk)

            lax.cond(jnp.min(bl2) >= -SAFE_LOG_DECAY, fast, slow)

    _loop(n_tiles, hg_tile)

    for d in range(2):
        for h in range(N_HEADS):
            if latent:
                st_s[d * N_HEADS + h] = s_hg_ref[d, h].T
            else:
                st_s[d * N_HEADS + h] = jnp.zeros((HEAD_W, HEAD_W), F32)

    inter_refs = (fa_s, fb_s)

    def hg_scan(j):
        for d in range(2):
            c = j if d == 0 else n_chunks - 1 - j
            rws = pl.ds(pl.multiple_of(c * HG_CHUNK, HG_CHUNK), HG_CHUNK)
            ebl = ebl_s[pl.ds(d * n_chunks + c, 1), :]
            for h in range(N_HEADS):
                hs = slice(h * HEAD_W, (h + 1) * HEAD_W)
                st = st_s[d * N_HEADS + h]
                inter_refs[d][rws, hs] = _dot_nt(hqb_s[d, rws, hs], st.astype(BF16))
                st_s[d * N_HEADS + h] = st * ebl[:, hs] + _dot_tn(hv_s[rws, hs], hkd_s[d, rws, hs])

    _loop(n_chunks, hg_scan, unroll=2)
    if not latent:
        for d in range(2):
            for h in range(N_HEADS):
                f_hg_ref[d, h] = st_s[d * N_HEADS + h].T

    def head_norm_gate(y, gain_ref, col0, rows):
        for h in range(N_HEADS):
            hs = slice(h * HEAD_W, (h + 1) * HEAD_W)
            yh = y[:, hs]
            ms = jnp.mean(yh * yh, axis=-1, keepdims=True)
            yn = yh * lax.rsqrt(ms + EPS) * gain_ref[:, hs]
            cs = slice(col0 + h * HEAD_W, col0 + (h + 1) * HEAD_W)
            ycat_s[rows, cs] = (yn * ycat_s[rows, cs].astype(F32)).astype(BF16)

    def hg_out_tile(i):
        rows = rows_of(i)
        head_norm_gate(fc_s[rows, :] + fa_s[rows, :] + fb_s[rows, :], hgn_ref, D_A, rows)

    _loop(n_tiles, hg_out_tile)

    def ret_tile(i):
        rows = rows_of(i)
        p = _dot(xn_s[rows, :], w_in_ref[:, COL_C:COL_C + 4 * D_C])
        ycat_s[rows, D_A + D_B:D_MIX] = _silu(p[:, 3 * D_C:4 * D_C]).astype(BF16)
        for h in range(N_HEADS):
            hs = slice(h * HEAD_W, (h + 1) * HEAD_W)
            qh = p[:, hs]
            kh = p[:, D_C + h * HEAD_W:D_C + (h + 1) * HEAD_W]
            vh = p[:, 2 * D_C + h * HEAD_W:2 * D_C + (h + 1) * HEAD_W].astype(BF16)
            if latent:
                cos = cos_ref[rows, :]
                sa = sina_ref[rows, :]
                sb = sinb_ref[rows, :]
                qh = qh * cos + pltpu.roll(qh, HEAD_W - 32, 1) * sa + pltpu.roll(qh, 32, 1) * sb
                kh = kh * cos + pltpu.roll(kh, HEAD_W - 32, 1) * sa + pltpu.roll(kh, 32, 1) * sb
            qh = qh * (RET_DK ** -0.5)
            qb = qh.astype(BF16)
            hq_s[rows, hs] = qb
            sc = _dot_nt(qb, kh.astype(BF16)) * ds_s[h]
            fa_s[rows, hs] = _dot(sc.astype(BF16), vh)
            for d in range(2):
                kw = (kh * wtab_s[d, h]).astype(BF16)
                kv_s[(i * 2 + d) * N_HEADS + h] = _dot_tn(kw, vh)

    _loop(n_tiles, ret_tile)

    carried = latent or n_tiles > 1
    for d in range(2):
        for h in range(N_HEADS):
            hs = slice(h * HEAD_W, (h + 1) * HEAD_W)
            if latent:
                state = s_ret_ref[d, h]
            else:
                state = jnp.zeros((HEAD_W, HEAD_W), F32)
            g_c = gc_s[d, h][0:1, :]
            order = range(n_tiles) if d == 0 else range(n_tiles - 1, -1, -1)
            for n in order:
                rows = pl.ds(n * TM, TM)
                if carried:
                    qw = (hq_s[rows, hs].astype(F32) * wtab_s[2 + d, h]).astype(BF16)
                    fa_s[rows, hs] = fa_s[rows, hs] + _dot(qw, state.astype(BF16))
                state = state * g_c + kv_s[(n * 2 + d) * N_HEADS + h]
            if not latent:
                f_ret_ref[d, h] = state

    def ret_out_tile(i):
        rows = rows_of(i)
        head_norm_gate(fa_s[rows, :], rtn_ref, D_A + D_B, rows)

    _loop(n_tiles, ret_out_tile)

    fg = fg_ref[...]

    def out_tile(i):
        rows = rows_of(i)
        delta = _dot(ycat_s[rows, :], w_out_ref[...])
        out = x_ref[rows, :] + gate * delta
        if last:
            ms = jnp.mean(out * out, axis=-1, keepdims=True)
            out = out * lax.rsqrt(ms + EPS) * fg
        y_ref[rows, :] = out

    _loop(n_tiles, out_tile)


def _const_spec(shape):
    zeros = (0,) * len(shape)
    return pl.BlockSpec(shape, lambda b: zeros, pipeline_mode=pl.Buffered(1))


def _layer_call(x, mod, layer, latent, last, prm, states, rope):
    bsz, seq, _ = x.shape
    n_tiles = seq // TM
    n_chunks = seq // HG_CHUNK
    per_seq = lambda *tail: pl.BlockSpec((None,) + tail, lambda b: (b,) + (0,) * len(tail))
    mod_spec = (per_seq(3, D_MODEL) if latent
                else pl.BlockSpec((None, 3, D_MODEL), lambda b: (0, 0, 0)))
    operands = [x, mod, prm["norm_g"], prm["w_in"], prm["conv_w"], prm["conv_b"], prm["wblk"], prm["gbias"],
                prm["lam"], prm["lb_logits"], prm["hgrn_norm_g"], prm["rdl"], prm["ret_norm_g"], prm["w_out"],
                prm["final_g"]]
    act_mode = dict(pipeline_mode=pl.Buffered(1)) if latent else {}
    act_spec = pl.BlockSpec((None, seq, D_MODEL), lambda b: (b, 0, 0), **act_mode)
    in_specs = [act_spec, mod_spec] + [_const_spec(o.shape) for o in operands[2:]]
    if latent:
        operands += list(states) + list(rope)
        in_specs += [per_seq(2, D_A), per_seq(2, N_HEADS, HEAD_W, HEAD_W), per_seq(2, N_HEADS, HEAD_W, HEAD_W)]
        in_specs += [_const_spec(r.shape) for r in rope]
    out_shape = [jax.ShapeDtypeStruct((bsz, seq, D_MODEL), F32)]
    out_specs = [act_spec]
    if not latent:
        out_shape += [jax.ShapeDtypeStruct((bsz, 2, D_A), F32),
                      jax.ShapeDtypeStruct((bsz, 2, N_HEADS, HEAD_W, HEAD_W), F32),
                      jax.ShapeDtypeStruct((bsz, 2, N_HEADS, HEAD_W, HEAD_W), F32)]
        out_specs += [per_seq(2, D_A), per_seq(2, N_HEADS, HEAD_W, HEAD_W), per_seq(2, N_HEADS, HEAD_W, HEAD_W)]
    scratch = [
        pltpu.VMEM((seq, D_MODEL), BF16),
        pltpu.VMEM((seq, D_MIX), BF16),
        pltpu.VMEM((seq + 16, D_A), F32),
        pltpu.VMEM((seq, D_A), F32), pltpu.VMEM((seq, D_A), F32),
        pltpu.VMEM((seq, D_A), F32), pltpu.VMEM((seq, D_A), F32),
        pltpu.VMEM((seq, D_B), BF16), pltpu.VMEM((seq, D_B), BF16),
        pltpu.VMEM((2, seq, D_B), BF16), pltpu.VMEM((2, seq, D_B), BF16),
        pltpu.VMEM((2 * n_chunks, D_B), F32),
        pltpu.VMEM((2 * N_HEADS, HEAD_W, HEAD_W), F32),
        pltpu.VMEM((N_HEADS, TM, TM), F32),
        pltpu.VMEM((4, N_HEADS, TM, HEAD_W), F32),
        pltpu.VMEM((2, N_HEADS, 8, HEAD_W), F32),
        pltpu.VMEM((n_tiles * 2 * N_HEADS, HEAD_W, HEAD_W), F32),
    ]
    body = functools.partial(_layer_kernel, seq=seq, layer=layer, latent=latent, last=last)
    return pl.pallas_call(
        body,
        out_shape=out_shape,
        grid=(bsz,),
        in_specs=in_specs,
        out_specs=out_specs,
        scratch_shapes=scratch,
        compiler_params=pltpu.CompilerParams(dimension_semantics=("arbitrary",),
                                             vmem_limit_bytes=V7X_VMEM_LIMIT_BYTES),
        name=("latent" if latent else "context") + f"_layer{layer}",
    )(*operands)


def _rope_tables(seq):
    half = RET_DK // 2
    nf = half // 2
    t = jnp.arange(seq)
    pos_r = (t // GRID_W).astype(F32)
    pos_c = (t % GRID_W).astype(F32)
    freq = ROPE_BASE ** (-jnp.arange(nf, dtype=F32) / nf)
    ang_r = pos_r[:, None] * freq[None, :]
    ang_c = pos_c[:, None] * freq[None, :]
    zero = jnp.zeros((seq, nf), F32)
    cos = jnp.concatenate([jnp.cos(ang_r)] * 2 + [jnp.cos(ang_c)] * 2, axis=-1)
    sin_a = jnp.concatenate([-jnp.sin(ang_r), zero, -jnp.sin(ang_c), zero], axis=-1)
    sin_b = jnp.concatenate([zero, jnp.sin(ang_r), zero, jnp.sin(ang_c)], axis=-1)
    return cos, sin_a, sin_b


def _block_diag_gates(wa, wx):
    per_half = GATE_HALF // LRU_BW
    eye = jnp.eye(per_half, dtype=F32)
    halves = []
    for hf in range(D_A // GATE_HALF):
        cols = []
        for d in range(2):
            for w in (wa, wx):
                blk = w[d, hf * per_half:(hf + 1) * per_half]
                dense = jnp.einsum('pq,pij->piqj', eye, blk).reshape(GATE_HALF, GATE_HALF)
                cols.append(dense)
        halves.append(jnp.concatenate(cols, axis=-1))
    return jnp.stack(halves).astype(BF16)


def kernel(x_prompt, x_sample, state_lru, state_hgrn, state_ret, c, c_ctx, norm_g, w_mod, b_mod, w_in, conv_w,
           conv_b, lru_wa, lru_ba, lru_wx, lru_bx, lru_lambda, hgrn_lb_logits, hgrn_norm_g, ret_decay_logit,
           ret_norm_g, w_out, final_g):
    dec_b = x_sample.shape[0]
    cond = jnp.zeros((COND_ROWS, D_MODEL), F32).at[0].set(c_ctx).at[1:1 + dec_b].set(c)
    mods = _modulation(cond, w_mod, b_mod).reshape(DEPTH, COND_ROWS, 3, D_MODEL)

    prms = []
    for l in range(DEPTH):
        prms.append(dict(
            norm_g=norm_g[l][None, :],
            w_in=w_in[l].astype(BF16),
            conv_w=conv_w[l],
            conv_b=conv_b[l][None, :],
            wblk=_block_diag_gates(lru_wa[l], lru_wx[l]),
            gbias=jnp.stack([lru_ba[l, 0], lru_bx[l, 0], lru_ba[l, 1], lru_bx[l, 1]]),
            lam=lru_lambda[l],
            lb_logits=hgrn_lb_logits.reshape(DEPTH * 2, D_B),
            hgrn_norm_g=hgrn_norm_g[l][None, :],
            rdl=jnp.broadcast_to(ret_decay_logit[l].reshape(2 * RET_HEADS, 1), (2 * RET_HEADS, TM)),
            ret_norm_g=ret_norm_g[l][None, :],
            w_out=w_out[l].astype(BF16),
            final_g=final_g[None, :],
        ))

    x = x_prompt
    lru_l, hg_l, ret_l = [], [], []
    for l in range(DEPTH):
        x, f_lru, f_hg, f_ret = _layer_call(x, mods[l, 0:1], l, False, l == DEPTH - 1, prms[l], None, None)
        lru_l.append(f_lru)
        hg_l.append(f_hg)
        ret_l.append(f_ret)
    y_prompt = x

    rope = _rope_tables(x_sample.shape[1])
    z = x_sample
    for l in range(DEPTH):
        states = (state_lru[:, l], state_hgrn[:, l], state_ret[:, l])
        (z,) = _layer_call(z, mods[l, 1:1 + dec_b], l, True, l == DEPTH - 1, prms[l], states, rope)
    y_sample = z

    return (y_prompt, y_sample, jnp.stack(lru_l, axis=1), jnp.stack(hg_l, axis=1), jnp.stack(ret_l, axis=1))
```

```python
import functools

import numpy as np

import jax
import jax.numpy as jnp
from jax import lax
from jax.experimental import pallas as pl
from jax.experimental.pallas import tpu as pltpu

F32 = jnp.float32
BF16 = jnp.bfloat16

D_MODEL = 1024
DEPTH = 2
GRID_W = 64
D_A = 512
LRU_BLOCKS = 8
LRU_BW = D_A // LRU_BLOCKS
CONV_W = 4
CONV_LEFT = 1
LRU_C = 8.0
D_B = 512
HG_HEADS = 4
HG_DK = D_B // HG_HEADS
HG_CHUNK = 32
F_MIN = 1e-20
D_C = 512
RET_HEADS = 4
RET_DK = D_C // RET_HEADS
ROPE_BASE = 10000.0
D_MIX = D_A + D_B + D_C
D_IN = 2 * D_A + 5 * D_B + 4 * D_C
EPS = 1e-6

COL_A = 0
COL_B = 2 * D_A
COL_C = COL_B + 5 * D_B

TM = 256
HEAD_W = 128
N_HEADS = 4
GATE_HALF = 256
SAFE_LOG_DECAY = 70.0
V7X_VMEM_LIMIT_BYTES = 60000 * 1024
COND_ROWS = 8


def _sigmoid(x):
    return jax.nn.sigmoid(x)


def _silu(x):
    return x * jax.nn.sigmoid(x)


def _dot(a, b):
    return jnp.dot(a, b, preferred_element_type=F32)


def _dot_nt(a, b):
    return lax.dot_general(a, b, (((1,), (1,)), ((), ())), preferred_element_type=F32)


def _dot_tn(a, b):
    return lax.dot_general(a, b, (((0,), (0,)), ((), ())), preferred_element_type=F32)


def _loop(n, body, unroll=1):
    if n == 1:
        body(0)
    else:
        def wrapped(i, carry):
            body(i)
            return carry
        lax.fori_loop(0, n, wrapped, 0, unroll=unroll)


def _mod_kernel(c_ref, w_ref, b_ref, o_ref):
    c = c_ref[...]
    o_ref[...] = _dot(_silu(c), w_ref[...]) + b_ref[...]


def _modulation(cond, w_mod, b_mod):
    n_col = 3 * D_MODEL
    blk = D_MODEL
    return pl.pallas_call(
        _mod_kernel,
        out_shape=jax.ShapeDtypeStruct((DEPTH, COND_ROWS, n_col), F32),
        grid=(DEPTH, n_col // blk),
        in_specs=[
            pl.BlockSpec((COND_ROWS, D_MODEL), lambda l, j: (0, 0)),
            pl.BlockSpec((None, D_MODEL, blk), lambda l, j: (l, 0, j)),
            pl.BlockSpec((None, 1, blk), lambda l, j: (l, 0, j)),
        ],
        out_specs=pl.BlockSpec((None, COND_ROWS, blk), lambda l, j: (l, 0, j)),
        name="modulation",
    )(cond, w_mod, b_mod.reshape(DEPTH, 1, n_col))


def _trunk_kernel(*refs, seq, layers, latent):
    n_tiles = seq // TM
    n_chunks = seq // HG_CHUNK
    n_local = len(layers)
    it = iter(refs)
    x_ref = next(it); mod_ref = next(it); ng_ref = next(it); w_in_ref = next(it)
    cw_ref = next(it); cb_ref = next(it); wblk_ref = next(it); gb_ref = next(it); lam_ref = next(it)
    lbl_ref = next(it); hgn_ref = next(it); rdl_ref = next(it); rtn_ref = next(it)
    w_out_ref = next(it); fg_ref = next(it)
    if latent:
        s_lru_ref = next(it); s_hg_ref = next(it); s_ret_ref = next(it)
        cos_ref = next(it); sina_ref = next(it); sinb_ref = next(it)
    y_ref = next(it)
    if not latent:
        f_lru_ref = next(it); f_hg_ref = next(it); f_ret_ref = next(it)
    xn_s = next(it); ycat_s = next(it); xpad_s = next(it)
    fa_s = next(it); fb_s = next(it); fc_s = next(it); fd_s = next(it)
    hv_s = next(it); hq_s = next(it); hqb_s = next(it); hkd_s = next(it)
    ebl_s = next(it); st_s = next(it)
    ds_s = next(it); wtab_s = next(it); gc_s = next(it); kv_s = next(it)
    xmid_s = next(it) if n_local > 1 else None

    def rows_of(i):
        return pl.ds(pl.multiple_of(i * TM, TM), TM)

    @pl.when(pl.program_id(0) == 0)
    def _():
        ti = lax.broadcasted_iota(jnp.int32, (TM, TM), 0)
        si = lax.broadcasted_iota(jnp.int32, (TM, TM), 1)
        rel = (ti - si).astype(F32)
        tcol = lax.broadcasted_iota(jnp.int32, (TM, HEAD_W), 0).astype(F32)
        for li in range(n_local):
            lg_all = -jax.nn.softplus(-rdl_ref[li])
            for h in range(N_HEADS):
                lg_f = lg_all[h:h + 1, :]
                lg_b = lg_all[N_HEADS + h:N_HEADS + h + 1, :]
                d_f = jnp.where(ti >= si, jnp.exp(jnp.where(ti >= si, rel, 0.0) * lg_f), 0.0)
                d_b = jnp.where(si >= ti, jnp.exp(jnp.where(si >= ti, -rel, 0.0) * lg_b), 0.0)
                ds_s[li, h] = d_f + d_b
                lgf = lg_f[:, :HEAD_W]
                lgb = lg_b[:, :HEAD_W]
                wtab_s[li, 0, h] = jnp.exp((TM - 1.0 - tcol) * lgf)
                wtab_s[li, 1, h] = jnp.exp(tcol * lgb)
                wtab_s[li, 2, h] = jnp.exp((tcol + 1.0) * lgf)
                wtab_s[li, 3, h] = jnp.exp((TM - tcol) * lgb)
                gc_s[li, 0, h] = jnp.broadcast_to(jnp.exp(TM * lgf), (8, HEAD_W))
                gc_s[li, 1, h] = jnp.broadcast_to(jnp.exp(TM * lgb), (8, HEAD_W))

    ri = lax.broadcasted_iota(jnp.int32, (TM, TM), 0)
    ci = lax.broadcasted_iota(jnp.int32, (TM, TM), 1)
    same_chunk = (ri // HG_CHUNK) == (ci // HG_CHUNK)
    keeps = (same_chunk & (ci <= ri), same_chunk & (ci >= ri))
    cum_mats = tuple(jnp.where(kp, 1.0, 0.0).astype(BF16) for kp in keeps)
    cpt = TM // HG_CHUNK
    rrow = lax.broadcasted_iota(jnp.int32, (HG_CHUNK, HEAD_W), 0)
    chunk_shape = (cpt, HG_CHUNK, D_B)
    last_row = (HG_CHUNK - 1, 0)
    mid_row = (HG_CHUNK // 2 - 1, HG_CHUNK // 2)
    lbl = lbl_ref[...]

    def run_layer(li, layer, src_ref, dst_ref):
        last = layer == DEPTH - 1
        shift = mod_ref[li, 0:1, :]
        scale = mod_ref[li, 1:2, :]
        gate = mod_ref[li, 2:3, :]

        ng = ng_ref[li]

        def norm_tile(i):
            rows = rows_of(i)
            x = src_ref[rows, :]
            ms = jnp.mean(x * x, axis=-1, keepdims=True)
            xn = (x * lax.rsqrt(ms + EPS)) * ng
            xn = xn * (1.0 + scale) + shift
            xn_s[rows, :] = xn.astype(BF16)

        _loop(n_tiles, norm_tile)

        zero_pad = jnp.zeros((8, D_A), F32)
        xpad_s[pl.ds(0, 8), :] = zero_pad
        xpad_s[pl.ds(seq + 8, 8), :] = zero_pad

        def lru_proj_tile(i):
            rows = rows_of(i)
            pa = _dot(xn_s[rows, :], w_in_ref[li, :, COL_A:COL_A + 2 * D_A])
            xpad_s[pl.ds(pl.multiple_of(i * TM + 8, 8), TM), :] = pa[:, :D_A]
            ycat_s[rows, 0:D_A] = _silu(pa[:, D_A:]).astype(BF16)

        _loop(n_tiles, lru_proj_tile)

        cw = cw_ref[li]
        cb = cb_ref[li]
        nsp = jax.nn.softplus(-lam_ref[li])
        a_refs = (fa_s, fc_s)
        b_refs = (fb_s, fd_s)

        def lru_gate_tile(i):
            rows = rows_of(i)
            win = xpad_s[pl.ds(pl.multiple_of(i * TM, TM), TM + 16), :]
            wn = TM + 16
            u = cb
            for k in range(CONV_W):
                off = k - CONV_LEFT
                sh = win if off == 0 else pltpu.roll(win, (-off) % wn, 0)
                u = u + sh[8:8 + TM, :] * cw[k:k + 1, :]
            for hf in range(D_A // GATE_HALF):
                cols = slice(hf * GATE_HALF, (hf + 1) * GATE_HALF)
                uh = u[:, cols]
                g4 = _dot(uh.astype(BF16), wblk_ref[li, hf])
                for d in range(2):
                    c0 = 2 * d * GATE_HALF
                    r = _sigmoid(g4[:, c0:c0 + GATE_HALF] + gb_ref[li, 2 * d:2 * d + 1, cols])
                    ig = _sigmoid(g4[:, c0 + GATE_HALF:c0 + 2 * GATE_HALF] + gb_ref[li, 2 * d + 1:2 * d + 2, cols])
                    log_a = (-LRU_C) * r * nsp[d:d + 1, cols]
                    th = jnp.tanh(log_a)
                    bcoef = jnp.sqrt((-2.0 * th) / (1.0 - th))
                    a_refs[d][rows, cols] = jnp.exp(log_a)
                    b_refs[d][rows, cols] = bcoef * (ig * uh)

        _loop(n_tiles, lru_gate_tile)

        if latent:
            h0f = s_lru_ref[li, 0:1, :]
            h0b = s_lru_ref[li, 1:2, :]
        else:
            h0f = jnp.zeros((1, D_A), F32)
            h0b = h0f

        def lru_step(t, carry):
            hf, hb = carry
            rf = pl.ds(t, 1)
            hf = fa_s[rf, :] * hf + fb_s[rf, :]
            fa_s[rf, :] = hf
            rb = pl.ds(seq - 1 - t, 1)
            hb = fc_s[rb, :] * hb + fd_s[rb, :]
            fc_s[rb, :] = hb
            return hf, hb

        hf_fin, hb_fin = lax.fori_loop(0, seq, lru_step, (h0f, h0b), unroll=8)
        if not latent:
            f_lru_ref[layer, 0:1, :] = hf_fin
            f_lru_ref[layer, 1:2, :] = hb_fin

        def lru_out_tile(i):
            rows = rows_of(i)
            ya = (fa_s[rows, :] + fc_s[rows, :]) * ycat_s[rows, 0:D_A].astype(F32)
            ycat_s[rows, 0:D_A] = ya.astype(BF16)

        _loop(n_tiles, lru_out_tile)

        lb_rows = []
        for d in range(2):
            ls = [lbl[k * 2 + d:k * 2 + d + 1, :] for k in range(DEPTH)]
            mx = functools.reduce(jnp.maximum, ls)
            es = [jnp.exp(v - mx) for v in ls]
            den = functools.reduce(lambda p, q: p + q, es)
            acc = jnp.zeros_like(mx)
            for k in range(1, layer + 1):
                acc = acc + es[k] / den
            lb_rows.append(acc)

        def hg_proj(i):
            return _dot(xn_s[rows_of(i), :], w_in_ref[li, :, COL_B:COL_B + 5 * D_B])

        def hg_gates(p, d):
            z = p[:, (1 + d) * D_B:(2 + d) * D_B]
            lb = lb_rows[d]
            sig = _sigmoid(z)
            f = lb + (1.0 - lb) * sig
            logf = jnp.log(jnp.maximum(f, F_MIN))
            k = (1.0 - lb) * (1.0 - sig)
            hi = logf.astype(BF16)
            lo = (logf - hi.astype(F32)).astype(BF16)
            b = _dot(cum_mats[d], hi) + _dot(cum_mats[d], lo)
            return k.reshape(chunk_shape), b.reshape(chunk_shape)

        def hg_tile(i):
            rows = rows_of(i)
            p = hg_proj(i)
            q3 = _silu(p[:, 0:D_B]).reshape(chunk_shape)
            v = p[:, 3 * D_B:4 * D_B].astype(BF16)
            hv_s[rows, :] = v
            ycat_s[rows, D_A:D_A + D_B] = _silu(p[:, 4 * D_B:5 * D_B]).astype(BF16)
            qmb, kib = [], []
            worst = None
            for d in range(2):
                k3, b3 = hg_gates(p, d)
                bl3 = b3[:, last_row[d]:last_row[d] + 1, :]
                m3 = b3[:, mid_row[d]:mid_row[d] + 1, :]
                rel = b3 - m3
                qm = q3 * jnp.exp(rel)
                ki = k3 * jnp.exp(-rel)
                hqb_s[d, rows, :] = (qm * jnp.exp(m3)).reshape(TM, D_B).astype(BF16)
                hkd_s[d, rows, :] = (ki * jnp.exp(bl3 - m3)).reshape(TM, D_B).astype(BF16)
                ebl_s[pl.ds(pl.multiple_of(d * n_chunks + i * cpt, cpt), cpt), :] = jnp.exp(bl3.reshape(cpt, D_B))
                span = jnp.maximum(jnp.max(-m3), jnp.max(m3 - bl3))
                worst = span if worst is None else jnp.maximum(worst, span)
                qmb.append(qm.reshape(TM, D_B).astype(BF16))
                kib.append(ki.reshape(TM, D_B).astype(BF16))
            for h in range(N_HEADS):
                hs = slice(h * HEAD_W, (h + 1) * HEAD_W)
                att = (jnp.where(keeps[0], _dot_nt(qmb[0][:, hs], kib[0][:, hs]), 0.0)
                       + jnp.where(keeps[1], _dot_nt(qmb[1][:, hs], kib[1][:, hs]), 0.0))
                fc_s[rows, hs] = _dot(att.astype(BF16), v[:, hs])

            @pl.when(worst > SAFE_LOG_DECAY)
            def _():
                hg_tile_direct(i)

        def hg_tile_direct(i):
            rows = rows_of(i)
            p = hg_proj(i)
            q3 = _silu(p[:, 0:D_B]).reshape(chunk_shape)
            fb_s[rows, :] = q3.reshape(TM, D_B)
            for d in range(2):
                k3, b3 = hg_gates(p, d)
                bl3 = b3[:, last_row[d]:last_row[d] + 1, :]
                hqb_s[d, rows, :] = (q3 * jnp.exp(b3)).reshape(TM, D_B).astype(BF16)
                hkd_s[d, rows, :] = (k3 * jnp.exp(bl3 - b3)).reshape(TM, D_B).astype(BF16)
                fa_s[rows, :] = b3.reshape(TM, D_B)
                fd_s[rows, :] = k3.reshape(TM, D_B)

                def chunk(c):
                    rws = pl.ds(pl.multiple_of(i * TM + c * HG_CHUNK, HG_CHUNK), HG_CHUNK)
                    for h in range(N_HEADS):
                        hs = slice(h * HEAD_W, (h + 1) * HEAD_W)
                        bf = fa_s[rws, hs]
                        qf = fb_s[rws, hs]
                        kf = fd_s[rws, hs]
                        vf = hv_s[rws, hs].astype(F32)

                        def src(s, acc):
                            pick = rrow == s
                            ks = jnp.sum(jnp.where(pick, kf, 0.0), axis=0, keepdims=True)
                            vs = jnp.sum(jnp.where(pick, vf, 0.0), axis=0, keepdims=True)
                            bs = jnp.sum(jnp.where(pick, bf, 0.0), axis=0, keepdims=True)
                            w = jnp.sum(qf * ks * jnp.exp(jnp.minimum(bf - bs, 0.0)), axis=-1, keepdims=True)
                            keep = (rrow[:, 0:1] >= s) if d == 0 else (rrow[:, 0:1] <= s)
                            return acc + jnp.where(keep, w, 0.0) * vs

                        o = lax.fori_loop(0, HG_CHUNK, src, jnp.zeros((HG_CHUNK, HEAD_W), F32))
                        fc_s[rws, hs] = o if d == 0 else fc_s[rws, hs] + o

                _loop(cpt, chunk)

        _loop(n_tiles, hg_tile)

        for d in range(2):
            for h in range(N_HEADS):
                if latent:
                    st_s[d * N_HEADS + h] = s_hg_ref[li, d, h].T
                else:
                    st_s[d * N_HEADS + h] = jnp.zeros((HEAD_W, HEAD_W), F32)

        inter_refs = (fa_s, fb_s)

        def hg_scan(j):
            for d in range(2):
                c = j if d == 0 else n_chunks - 1 - j
                rws = pl.ds(pl.multiple_of(c * HG_CHUNK, HG_CHUNK), HG_CHUNK)
                ebl = ebl_s[pl.ds(d * n_chunks + c, 1), :]
                for h in range(N_HEADS):
                    hs = slice(h * HEAD_W, (h + 1) * HEAD_W)
                    st = st_s[d * N_HEADS + h]
                    inter_refs[d][rws, hs] = _dot_nt(hqb_s[d, rws, hs], st.astype(BF16))
                    st_s[d * N_HEADS + h] = st * ebl[:, hs] + _dot_tn(hv_s[rws, hs], hkd_s[d, rws, hs])

        _loop(n_chunks, hg_scan, unroll=2)
        if not latent:
            for d in range(2):
                for h in range(N_HEADS):
                    f_hg_ref[layer, d, h] = st_s[d * N_HEADS + h].T

        def head_norm_gate(y, gain, col0, rows):
            for h in range(N_HEADS):
                hs = slice(h * HEAD_W, (h + 1) * HEAD_W)
                yh = y[:, hs]
                ms = jnp.mean(yh * yh, axis=-1, keepdims=True)
                yn = yh * lax.rsqrt(ms + EPS) * gain[:, hs]
                cs = slice(col0 + h * HEAD_W, col0 + (h + 1) * HEAD_W)
                ycat_s[rows, cs] = (yn * ycat_s[rows, cs].astype(F32)).astype(BF16)

        hgn = hgn_ref[li]

        def hg_out_tile(i):
            rows = rows_of(i)
            head_norm_gate(fc_s[rows, :] + fa_s[rows, :] + fb_s[rows, :], hgn, D_A, rows)

        _loop(n_tiles, hg_out_tile)

        def ret_tile(i):
            rows = rows_of(i)
            p = _dot(xn_s[rows, :], w_in_ref[li, :, COL_C:COL_C + 4 * D_C])
            ycat_s[rows, D_A + D_B:D_MIX] = _silu(p[:, 3 * D_C:4 * D_C]).astype(BF16)
            for h in range(N_HEADS):
                hs = slice(h * HEAD_W, (h + 1) * HEAD_W)
                qh = p[:, hs]
                kh = p[:, D_C + h * HEAD_W:D_C + (h + 1) * HEAD_W]
                vh = p[:, 2 * D_C + h * HEAD_W:2 * D_C + (h + 1) * HEAD_W].astype(BF16)
                if latent:
                    cos = cos_ref[rows, :]
                    sa = sina_ref[rows, :]
                    sb = sinb_ref[rows, :]
                    qh = qh * cos + pltpu.roll(qh, HEAD_W - 32, 1) * sa + pltpu.roll(qh, 32, 1) * sb
                    kh = kh * cos + pltpu.roll(kh, HEAD_W - 32, 1) * sa + pltpu.roll(kh, 32, 1) * sb
                qh = qh * (RET_DK ** -0.5)
                qb = qh.astype(BF16)
                hq_s[rows, hs] = qb
                sc = _dot_nt(qb, kh.astype(BF16)) * ds_s[li, h]
                fa_s[rows, hs] = _dot(sc.astype(BF16), vh)
                for d in range(2):
                    kw = (kh * wtab_s[li, d, h]).astype(BF16)
                    kv_s[(i * 2 + d) * N_HEADS + h] = _dot_tn(kw, vh)

        _loop(n_tiles, ret_tile)

        carried = latent or n_tiles > 1
        for d in range(2):
            for h in range(N_HEADS):
                hs = slice(h * HEAD_W, (h + 1) * HEAD_W)
                if latent:
                    state = s_ret_ref[li, d, h]
                else:
                    state = jnp.zeros((HEAD_W, HEAD_W), F32)
                g_c = gc_s[li, d, h][0:1, :]
                order = range(n_tiles) if d == 0 else range(n_tiles - 1, -1, -1)
                for n in order:
                    rows = pl.ds(n * TM, TM)
                    if carried:
                        qw = (hq_s[rows, hs].astype(F32) * wtab_s[li, 2 + d, h]).astype(BF16)
                        fa_s[rows, hs] = fa_s[rows, hs] + _dot(qw, state.astype(BF16))
                    state = state * g_c + kv_s[(n * 2 + d) * N_HEADS + h]
                if not latent:
                    f_ret_ref[layer, d, h] = state

        rtn = rtn_ref[li]

        def ret_out_tile(i):
            rows = rows_of(i)
            head_norm_gate(fa_s[rows, :], rtn, D_A + D_B, rows)

        _loop(n_tiles, ret_out_tile)

        fg = fg_ref[...]

        def out_tile(i):
            rows = rows_of(i)
            delta = _dot(ycat_s[rows, :], w_out_ref[li])
            out = src_ref[rows, :] + gate * delta
            if last:
                ms = jnp.mean(out * out, axis=-1, keepdims=True)
                out = out * lax.rsqrt(ms + EPS) * fg
            dst_ref[rows, :] = out

        _loop(n_tiles, out_tile)

    for li, layer in enumerate(layers):
        src = x_ref if li == 0 else xmid_s
        dst = y_ref if li == n_local - 1 else xmid_s
        run_layer(li, layer, src, dst)


def _trunk_call(x, mods, layers, latent, prm, states, rope):
    bsz, seq, _ = x.shape
    n_tiles = seq // TM
    n_chunks = seq // HG_CHUNK
    n_local = len(layers)
    l0 = layers[0]
    state_tail = (2, N_HEADS, HEAD_W, HEAD_W)

    def layers_spec(arr):
        tail = arr.shape[1:]
        zeros = (0,) * len(tail)
        return pl.BlockSpec((n_local,) + tail, lambda b: (l0 // n_local,) + zeros, pipeline_mode=pl.Buffered(1))

    def whole_spec(arr):
        zeros = (0,) * arr.ndim
        return pl.BlockSpec(arr.shape, lambda b: zeros, pipeline_mode=pl.Buffered(1))

    def per_seq_layers(tail, n_lay, first):
        zeros = (0,) * len(tail)
        return pl.BlockSpec((None, n_lay) + tail, lambda b: (b, first // n_lay) + zeros)

    act_mode = dict(pipeline_mode=pl.Buffered(1)) if latent else {}
    act_spec = pl.BlockSpec((None, seq, D_MODEL), lambda b: (b, 0, 0), **act_mode)
    cond_row = (lambda b: (l0 // n_local, 1 + b, 0, 0)) if latent else (lambda b: (l0 // n_local, 0, 0, 0))
    mod_spec = pl.BlockSpec((n_local, None, 3, D_MODEL), cond_row)

    stacked = [prm[k] for k in ("norm_g", "w_in", "conv_w", "conv_b", "wblk", "gbias", "lam")]
    operands = [x, mods] + stacked + [prm["lb_logits"], prm["hgrn_norm_g"], prm["rdl"], prm["ret_norm_g"],
                                      prm["w_out"], prm["final_g"]]
    in_specs = ([act_spec, mod_spec] + [layers_spec(a) for a in stacked]
                + [whole_spec(prm["lb_logits"]), layers_spec(prm["hgrn_norm_g"]), layers_spec(prm["rdl"]),
                   layers_spec(prm["ret_norm_g"]), layers_spec(prm["w_out"]), whole_spec(prm["final_g"])])
    if latent:
        operands += list(states) + list(rope)
        in_specs += [per_seq_layers((2, D_A), n_local, l0), per_seq_layers(state_tail, n_local, l0),
                     per_seq_layers(state_tail, n_local, l0)]
        in_specs += [whole_spec(r) for r in rope]
    out_shape = [jax.ShapeDtypeStruct((bsz, seq, D_MODEL), F32)]
    out_specs = [act_spec]
    if not latent:
        assert n_local == DEPTH, "the state outputs are written whole, one call must cover every layer"
        out_shape += [jax.ShapeDtypeStruct((bsz, DEPTH, 2, D_A), F32),
                      jax.ShapeDtypeStruct((bsz, DEPTH) + state_tail, F32),
                      jax.ShapeDtypeStruct((bsz, DEPTH) + state_tail, F32)]
        out_specs += [per_seq_layers((2, D_A), DEPTH, 0), per_seq_layers(state_tail, DEPTH, 0),
                      per_seq_layers(state_tail, DEPTH, 0)]
    scratch = [
        pltpu.VMEM((seq, D_MODEL), BF16),
        pltpu.VMEM((seq, D_MIX), BF16),
        pltpu.VMEM((seq + 16, D_A), F32),
        pltpu.VMEM((seq, D_A), F32), pltpu.VMEM((seq, D_A), F32),
        pltpu.VMEM((seq, D_A), F32), pltpu.VMEM((seq, D_A), F32),
        pltpu.VMEM((seq, D_B), BF16), pltpu.VMEM((seq, D_B), BF16),
        pltpu.VMEM((2, seq, D_B), BF16), pltpu.VMEM((2, seq, D_B), BF16),
        pltpu.VMEM((2 * n_chunks, D_B), F32),
        pltpu.VMEM((2 * N_HEADS, HEAD_W, HEAD_W), F32),
        pltpu.VMEM((n_local, N_HEADS, TM, TM), F32),
        pltpu.VMEM((n_local, 4, N_HEADS, TM, HEAD_W), F32),
        pltpu.VMEM((n_local, 2, N_HEADS, 8, HEAD_W), F32),
        pltpu.VMEM((n_tiles * 2 * N_HEADS, HEAD_W, HEAD_W), F32),
    ]
    if n_local > 1:
        scratch.append(pltpu.VMEM((seq, D_MODEL), F32))
    body = functools.partial(_trunk_kernel, seq=seq, layers=tuple(layers), latent=latent)
    return pl.pallas_call(
        body,
        out_shape=out_shape,
        grid=(bsz,),
        in_specs=in_specs,
        out_specs=out_specs,
        scratch_shapes=scratch,
        compiler_params=pltpu.CompilerParams(dimension_semantics=("arbitrary",),
                                             vmem_limit_bytes=V7X_VMEM_LIMIT_BYTES),
        name=("latent" if latent else "context") + "_layers" + "".join(str(l) for l in layers),
    )(*operands)


def _rope_tables(seq):
    half = RET_DK // 2
    nf = half // 2
    t = np.arange(seq)
    freq = (ROPE_BASE ** (-np.arange(nf, dtype=np.float32) / nf)).astype(np.float32)
    ang_r = (t // GRID_W).astype(np.float32)[:, None] * freq[None, :]
    ang_c = (t % GRID_W).astype(np.float32)[:, None] * freq[None, :]
    zero = np.zeros((seq, nf), np.float32)
    cos = np.concatenate([np.cos(ang_r)] * 2 + [np.cos(ang_c)] * 2, axis=-1)
    sin_a = np.concatenate([-np.sin(ang_r), zero, -np.sin(ang_c), zero], axis=-1)
    sin_b = np.concatenate([zero, np.sin(ang_r), zero, np.sin(ang_c)], axis=-1)
    return tuple(jnp.asarray(a, F32) for a in (cos, sin_a, sin_b))


def _block_diag_gates(wa, wx):
    per_half = GATE_HALF // LRU_BW
    n_half = D_A // GATE_HALF
    eye = jnp.eye(per_half, dtype=F32)
    cols = []
    for d in range(2):
        for w in (wa, wx):
            blk = w[:, d].reshape(DEPTH, n_half, per_half, LRU_BW, LRU_BW)
            dense = jnp.einsum('pq,lhpij->lhpiqj', eye, blk).reshape(DEPTH, n_half, GATE_HALF, GATE_HALF)
            cols.append(dense)
    return jnp.concatenate(cols, axis=-1).astype(BF16)


def kernel(x_prompt, x_sample, state_lru, state_hgrn, state_ret, c, c_ctx, norm_g, w_mod, b_mod, w_in, conv_w,
           conv_b, lru_wa, lru_ba, lru_wx, lru_bx, lru_lambda, hgrn_lb_logits, hgrn_norm_g, ret_decay_logit,
           ret_norm_g, w_out, final_g):
    dec_b = x_sample.shape[0]
    pad = jnp.zeros((COND_ROWS - 1 - dec_b, D_MODEL), F32)
    cond = jnp.concatenate([c_ctx[None, :], c, pad], axis=0)
    mods = _modulation(cond, w_mod, b_mod).reshape(DEPTH, COND_ROWS, 3, D_MODEL)

    prm = dict(
        norm_g=norm_g[:, None, :],
        w_in=w_in.astype(BF16),
        conv_w=conv_w,
        conv_b=conv_b[:, None, :],
        wblk=_block_diag_gates(lru_wa, lru_wx),
        gbias=jnp.stack([lru_ba[:, 0], lru_bx[:, 0], lru_ba[:, 1], lru_bx[:, 1]], axis=1),
        lam=lru_lambda,
        lb_logits=hgrn_lb_logits.reshape(DEPTH * 2, D_B),
        hgrn_norm_g=hgrn_norm_g[:, None, :],
        rdl=jnp.broadcast_to(ret_decay_logit.reshape(DEPTH, 2 * RET_HEADS, 1), (DEPTH, 2 * RET_HEADS, TM)),
        ret_norm_g=ret_norm_g[:, None, :],
        w_out=w_out.astype(BF16),
        final_g=final_g[None, :],
    )

    y_prompt, new_lru, new_hgrn, new_ret = _trunk_call(x_prompt, mods, list(range(DEPTH)), False, prm, None, None)

    rope = _rope_tables(x_sample.shape[1])
    z = x_sample
    for l in range(DEPTH):
        (z,) = _trunk_call(z, mods, [l], True, prm, (state_lru, state_hgrn, state_ret), rope)
    y_sample = z

    return (y_prompt, y_sample, new_lru, new_hgrn, new_ret)
```

```python
import functools

import numpy as np

import jax
import jax.numpy as jnp
from jax import lax
from jax.experimental import pallas as pl
from jax.experimental.pallas import tpu as pltpu

F32 = jnp.float32
BF16 = jnp.bfloat16

D_MODEL = 1024
DEPTH = 2
GRID_W = 64
D_A = 512
LRU_BLOCKS = 8
LRU_BW = D_A // LRU_BLOCKS
CONV_W = 4
CONV_LEFT = 1
LRU_C = 8.0
D_B = 512
HG_HEADS = 4
HG_DK = D_B // HG_HEADS
HG_CHUNK = 32
F_MIN = 1e-20
D_C = 512
RET_HEADS = 4
RET_DK = D_C // RET_HEADS
ROPE_BASE = 10000.0
D_MIX = D_A + D_B + D_C
D_IN = 2 * D_A + 5 * D_B + 4 * D_C
EPS = 1e-6

COL_A = 0
COL_B = 2 * D_A
COL_C = COL_B + 5 * D_B

TM = 256
HEAD_W = 128
N_HEADS = 4
GATE_HALF = 256
SAFE_LOG_DECAY = 70.0
V7X_VMEM_LIMIT_BYTES = 60000 * 1024
COND_ROWS = 8


def _sigmoid(x):
    return jax.nn.sigmoid(x)


def _silu(x):
    return x * jax.nn.sigmoid(x)


def _dot(a, b):
    return jnp.dot(a, b, preferred_element_type=F32)


def _dot_nt(a, b):
    return lax.dot_general(a, b, (((1,), (1,)), ((), ())), preferred_element_type=F32)


def _dot_tn(a, b):
    return lax.dot_general(a, b, (((0,), (0,)), ((), ())), preferred_element_type=F32)


def _loop(n, body, unroll=1):
    if n == 1:
        body(0)
    else:
        def wrapped(i, carry):
            body(i)
            return carry
        lax.fori_loop(0, n, wrapped, 0, unroll=unroll)


def _mod_kernel(c_ref, w_ref, b_ref, o_ref):
    c = c_ref[...]
    o_ref[...] = _dot(_silu(c), w_ref[...]) + b_ref[...]


def _modulation(cond, w_mod, b_mod):
    n_col = 3 * D_MODEL
    blk = D_MODEL
    return pl.pallas_call(
        _mod_kernel,
        out_shape=jax.ShapeDtypeStruct((DEPTH, COND_ROWS, n_col), F32),
        grid=(DEPTH, n_col // blk),
        in_specs=[
            pl.BlockSpec((COND_ROWS, D_MODEL), lambda l, j: (0, 0)),
            pl.BlockSpec((None, D_MODEL, blk), lambda l, j: (l, 0, j)),
            pl.BlockSpec((None, 1, blk), lambda l, j: (l, 0, j)),
        ],
        out_specs=pl.BlockSpec((None, COND_ROWS, blk), lambda l, j: (l, 0, j)),
        name="modulation",
    )(cond, w_mod, b_mod.reshape(DEPTH, 1, n_col))


def _trunk_kernel(*refs, seq, layers, latent, merged):
    n_tiles = seq // TM
    n_chunks = seq // HG_CHUNK
    n_local = len(layers)
    ret_carried = latent or n_tiles > 1
    it = iter(refs)
    x_ref = next(it); mod_ref = next(it); ng_ref = next(it); w_in_ref = next(it)
    cw_ref = next(it); cb_ref = next(it); wblk_ref = next(it); gb_ref = next(it); lam_ref = next(it)
    lbl_ref = next(it); hgn_ref = next(it); rdl_ref = next(it); rtn_ref = next(it)
    w_out_ref = next(it); fg_ref = next(it)
    if latent:
        s_lru_ref = next(it); s_hg_ref = next(it); s_ret_ref = next(it)
        cos_ref = next(it); sina_ref = next(it); sinb_ref = next(it)
    y_ref = next(it)
    if not latent:
        f_lru_ref = next(it); f_hg_ref = next(it); f_ret_ref = next(it)
    xn_s = next(it); ycat_s = next(it); xpad_s = next(it)
    fa_s = next(it); fb_s = next(it); fc_s = next(it); fd_s = next(it)
    hv_s = next(it); hq_s = next(it); hqb_s = next(it); hkd_s = next(it)
    ebl_s = next(it); st_s = next(it)
    ds_s = next(it); wtab_s = next(it); gc_s = next(it); kv_s = next(it)
    xmid_s = next(it) if n_local > 1 else None
    if merged:
        hg_intra_s = next(it); hg_if_s = next(it); hg_ib_s = next(it); ret_s = next(it)
        flag_s = next(it)
        tmp_b_s, tmp_q_s, tmp_k_s = hg_if_s, hg_ib_s, xpad_s
    else:
        hg_intra_s, hg_if_s, hg_ib_s, ret_s = fc_s, fa_s, fb_s, fa_s
        tmp_b_s, tmp_q_s, tmp_k_s = fa_s, fb_s, fd_s

    def rows_of(i):
        return pl.ds(pl.multiple_of(i * TM, TM), TM)

    @pl.when(pl.program_id(0) == 0)
    def _():
        ti = lax.broadcasted_iota(jnp.int32, (TM, TM), 0)
        si = lax.broadcasted_iota(jnp.int32, (TM, TM), 1)
        rel = (ti - si).astype(F32)
        tcol = lax.broadcasted_iota(jnp.int32, (TM, HEAD_W), 0).astype(F32)
        for li in range(n_local):
            lg_all = -jax.nn.softplus(-rdl_ref[li])
            for h in range(N_HEADS):
                lg_f = lg_all[h:h + 1, :]
                lg_b = lg_all[N_HEADS + h:N_HEADS + h + 1, :]
                d_f = jnp.where(ti >= si, jnp.exp(jnp.where(ti >= si, rel, 0.0) * lg_f), 0.0)
                d_b = jnp.where(si >= ti, jnp.exp(jnp.where(si >= ti, -rel, 0.0) * lg_b), 0.0)
                ds_s[li, h] = d_f + d_b
                lgf = lg_f[:, :HEAD_W]
                lgb = lg_b[:, :HEAD_W]
                wtab_s[li, 0, h] = jnp.exp((TM - 1.0 - tcol) * lgf)
                wtab_s[li, 1, h] = jnp.exp(tcol * lgb)
                if ret_carried:
                    wtab_s[li, 2, h] = jnp.exp((tcol + 1.0) * lgf)
                    wtab_s[li, 3, h] = jnp.exp((TM - tcol) * lgb)
                gc_s[li, 0, h] = jnp.broadcast_to(jnp.exp(TM * lgf), (8, HEAD_W))
                gc_s[li, 1, h] = jnp.broadcast_to(jnp.exp(TM * lgb), (8, HEAD_W))

    ri = lax.broadcasted_iota(jnp.int32, (TM, TM), 0)
    ci = lax.broadcasted_iota(jnp.int32, (TM, TM), 1)
    same_chunk = (ri // HG_CHUNK) == (ci // HG_CHUNK)
    keeps = (same_chunk & (ci <= ri), same_chunk & (ci >= ri))
    cum_mats = tuple(jnp.where(kp, 1.0, 0.0).astype(BF16) for kp in keeps)
    cpt = TM // HG_CHUNK
    rrow = lax.broadcasted_iota(jnp.int32, (HG_CHUNK, HEAD_W), 0)
    chunk_shape = (cpt, HG_CHUNK, D_B)
    last_row = (HG_CHUNK - 1, 0)
    mid_row = (HG_CHUNK // 2 - 1, HG_CHUNK // 2)
    lbl = lbl_ref[...]

    def run_layer(li, layer, src_ref, dst_ref):
        last = layer == DEPTH - 1
        shift = mod_ref[li, 0:1, :]
        scale = mod_ref[li, 1:2, :]
        gate = mod_ref[li, 2:3, :]

        ng = ng_ref[li]

        def norm_tile(i):
            rows = rows_of(i)
            x = src_ref[rows, :]
            ms = jnp.mean(x * x, axis=-1, keepdims=True)
            xn = (x * lax.rsqrt(ms + EPS)) * ng
            xn = xn * (1.0 + scale) + shift
            xn_s[rows, :] = xn.astype(BF16)

        def lru_proj_tile(i):
            rows = rows_of(i)
            pa = _dot(xn_s[rows, :], w_in_ref[li, :, COL_A:COL_A + 2 * D_A])
            xpad_s[pl.ds(pl.multiple_of(i * TM + 8, 8), TM), :] = pa[:, :D_A]
            ycat_s[rows, 0:D_A] = _silu(pa[:, D_A:]).astype(BF16)

        cw = cw_ref[li]
        cb = cb_ref[li]
        nsp = jax.nn.softplus(-lam_ref[li])
        a_refs = (fa_s, fc_s)
        b_refs = (fb_s, fd_s)

        def lru_gate_tile(i):
            rows = rows_of(i)
            win = xpad_s[pl.ds(pl.multiple_of(i * TM, TM), TM + 16), :]
            wn = TM + 16
            u = cb
            for k in range(CONV_W):
                off = k - CONV_LEFT
                sh = win if off == 0 else pltpu.roll(win, (-off) % wn, 0)
                u = u + sh[8:8 + TM, :] * cw[k:k + 1, :]
            for hf in range(D_A // GATE_HALF):
                cols = slice(hf * GATE_HALF, (hf + 1) * GATE_HALF)
                uh = u[:, cols]
                g4 = _dot(uh.astype(BF16), wblk_ref[li, hf])
                for d in range(2):
                    c0 = 2 * d * GATE_HALF
                    r = _sigmoid(g4[:, c0:c0 + GATE_HALF] + gb_ref[li, 2 * d:2 * d + 1, cols])
                    ig = _sigmoid(g4[:, c0 + GATE_HALF:c0 + 2 * GATE_HALF] + gb_ref[li, 2 * d + 1:2 * d + 2, cols])
                    log_a = (-LRU_C) * r * nsp[d:d + 1, cols]
                    th = jnp.tanh(log_a)
                    bcoef = jnp.sqrt((-2.0 * th) / (1.0 - th))
                    a_refs[d][rows, cols] = jnp.exp(log_a)
                    b_refs[d][rows, cols] = bcoef * (ig * uh)

        if latent:
            h0f = s_lru_ref[li, 0:1, :]
            h0b = s_lru_ref[li, 1:2, :]
        else:
            h0f = jnp.zeros((1, D_A), F32)
            h0b = h0f

        def lru_steps(g, carry):
            hf, hb = carry
            tf = pl.multiple_of(g * 8, 8)
            tb = pl.multiple_of(seq - 8 - g * 8, 8)
            for k in range(8):
                rf = pl.ds(tf + k, 1)
                hf = fa_s[rf, :] * hf + fb_s[rf, :]
                fa_s[rf, :] = hf
                rb = pl.ds(tb + 7 - k, 1)
                hb = fc_s[rb, :] * hb + fd_s[rb, :]
                fc_s[rb, :] = hb
            return hf, hb

        def lru_finish(carry):
            if not latent:
                f_lru_ref[layer, 0:1, :] = carry[0]
                f_lru_ref[layer, 1:2, :] = carry[1]

        def lru_out_tile(i):
            rows = rows_of(i)
            ya = (fa_s[rows, :] + fc_s[rows, :]) * ycat_s[rows, 0:D_A].astype(F32)
            ycat_s[rows, 0:D_A] = ya.astype(BF16)

        lb_rows = []
        for d in range(2):
            ls = [lbl[k * 2 + d:k * 2 + d + 1, :] for k in range(DEPTH)]
            mx = functools.reduce(jnp.maximum, ls)
            es = [jnp.exp(v - mx) for v in ls]
            den = functools.reduce(lambda p, q: p + q, es)
            acc = jnp.zeros_like(mx)
            for k in range(1, layer + 1):
                acc = acc + es[k] / den
            lb_rows.append(acc)

        def hg_proj(i):
            return _dot(xn_s[rows_of(i), :], w_in_ref[li, :, COL_B:COL_B + 5 * D_B])

        def hg_gates(p, d):
            z = p[:, (1 + d) * D_B:(2 + d) * D_B]
            lb = lb_rows[d]
            sig = _sigmoid(z)
            f = lb + (1.0 - lb) * sig
            logf = jnp.log(jnp.maximum(f, F_MIN))
            k = (1.0 - lb) * (1.0 - sig)
            hi = logf.astype(BF16)
            lo = (logf - hi.astype(F32)).astype(BF16)
            b = _dot(cum_mats[d], hi) + _dot(cum_mats[d], lo)
            return k.reshape(chunk_shape), b.reshape(chunk_shape)

        def hg_tile(i):
            rows = rows_of(i)
            p = hg_proj(i)
            q3 = _silu(p[:, 0:D_B]).reshape(chunk_shape)
            v = p[:, 3 * D_B:4 * D_B].astype(BF16)
            hv_s[rows, :] = v
            ycat_s[rows, D_A:D_A + D_B] = _silu(p[:, 4 * D_B:5 * D_B]).astype(BF16)
            qmb, kib = [], []
            worst = None
            for d in range(2):
                k3, b3 = hg_gates(p, d)
                bl3 = b3[:, last_row[d]:last_row[d] + 1, :]
                m3 = b3[:, mid_row[d]:mid_row[d] + 1, :]
                rel = b3 - m3
                qm = q3 * jnp.exp(rel)
                ki = k3 * jnp.exp(-rel)
                hqb_s[d, rows, :] = (qm * jnp.exp(m3)).reshape(TM, D_B).astype(BF16)
                hkd_s[d, rows, :] = (ki * jnp.exp(bl3 - m3)).reshape(TM, D_B).astype(BF16)
                ebl_s[pl.ds(pl.multiple_of(d * n_chunks + i * cpt, cpt), cpt), :] = jnp.exp(bl3.reshape(cpt, D_B))
                span = jnp.maximum(jnp.max(-m3), jnp.max(m3 - bl3))
                worst = span if worst is None else jnp.maximum(worst, span)
                qmb.append(qm.reshape(TM, D_B).astype(BF16))
                kib.append(ki.reshape(TM, D_B).astype(BF16))
            for h in range(N_HEADS):
                hs = slice(h * HEAD_W, (h + 1) * HEAD_W)
                att = (jnp.where(keeps[0], _dot_nt(qmb[0][:, hs], kib[0][:, hs]), 0.0)
                       + jnp.where(keeps[1], _dot_nt(qmb[1][:, hs], kib[1][:, hs]), 0.0))
                hg_intra_s[rows, hs] = _dot(att.astype(BF16), v[:, hs])
            return worst > SAFE_LOG_DECAY

        def hg_tile_direct(i):
            rows = rows_of(i)
            p = hg_proj(i)
            q3 = _silu(p[:, 0:D_B]).reshape(chunk_shape)
            tmp_q_s[rows, :] = q3.reshape(TM, D_B)
            for d in range(2):
                k3, b3 = hg_gates(p, d)
                bl3 = b3[:, last_row[d]:last_row[d] + 1, :]
                hqb_s[d, rows, :] = (q3 * jnp.exp(b3)).reshape(TM, D_B).astype(BF16)
                hkd_s[d, rows, :] = (k3 * jnp.exp(bl3 - b3)).reshape(TM, D_B).astype(BF16)
                tmp_b_s[rows, :] = b3.reshape(TM, D_B)
                tmp_k_s[rows, :] = k3.reshape(TM, D_B)

                def chunk(c):
                    rws = pl.ds(pl.multiple_of(i * TM + c * HG_CHUNK, HG_CHUNK), HG_CHUNK)
                    for h in range(N_HEADS):
                        hs = slice(h * HEAD_W, (h + 1) * HEAD_W)
                        bf = tmp_b_s[rws, hs]
                        qf = tmp_q_s[rws, hs]
                        kf = tmp_k_s[rws, hs]
                        vf = hv_s[rws, hs].astype(F32)

                        def src(s, acc):
                            pick = rrow == s
                            ks = jnp.sum(jnp.where(pick, kf, 0.0), axis=0, keepdims=True)
                            vs = jnp.sum(jnp.where(pick, vf, 0.0), axis=0, keepdims=True)
                            bs = jnp.sum(jnp.where(pick, bf, 0.0), axis=0, keepdims=True)
                            w = jnp.sum(qf * ks * jnp.exp(jnp.minimum(bf - bs, 0.0)), axis=-1, keepdims=True)
                            keep = (rrow[:, 0:1] >= s) if d == 0 else (rrow[:, 0:1] <= s)
                            return acc + jnp.where(keep, w, 0.0) * vs

                        o = lax.fori_loop(0, HG_CHUNK, src, jnp.zeros((HG_CHUNK, HEAD_W), F32))
                        hg_intra_s[rws, hs] = o if d == 0 else hg_intra_s[rws, hs] + o

                _loop(cpt, chunk)

        def hg_state_init():
            for d in range(2):
                for h in range(N_HEADS):
                    if latent:
                        st_s[d * N_HEADS + h] = s_hg_ref[li, d, h].T
                    else:
                        st_s[d * N_HEADS + h] = jnp.zeros((HEAD_W, HEAD_W), F32)

        inter_refs = (hg_if_s, hg_ib_s)

        def hg_scan(j):
            for d in range(2):
                c = j if d == 0 else n_chunks - 1 - j
                rws = pl.ds(pl.multiple_of(c * HG_CHUNK, HG_CHUNK), HG_CHUNK)
                ebl = ebl_s[pl.ds(d * n_chunks + c, 1), :]
                for h in range(N_HEADS):
                    hs = slice(h * HEAD_W, (h + 1) * HEAD_W)
                    st = st_s[d * N_HEADS + h]
                    inter_refs[d][rws, hs] = _dot_nt(hqb_s[d, rws, hs], st.astype(BF16))
                    st_s[d * N_HEADS + h] = st * ebl[:, hs] + _dot_tn(hv_s[rws, hs], hkd_s[d, rws, hs])

        def hg_finish():
            if not latent:
                for d in range(2):
                    for h in range(N_HEADS):
                        f_hg_ref[layer, d, h] = st_s[d * N_HEADS + h].T

        def head_norm_gate(y, gain, col0, rows):
            for h in range(N_HEADS):
                hs = slice(h * HEAD_W, (h + 1) * HEAD_W)
                yh = y[:, hs]
                ms = jnp.mean(yh * yh, axis=-1, keepdims=True)
                yn = yh * lax.rsqrt(ms + EPS) * gain[:, hs]
                cs = slice(col0 + h * HEAD_W, col0 + (h + 1) * HEAD_W)
                ycat_s[rows, cs] = (yn * ycat_s[rows, cs].astype(F32)).astype(BF16)

        hgn = hgn_ref[li]

        def hg_out_tile(i):
            rows = rows_of(i)
            head_norm_gate(hg_intra_s[rows, :] + hg_if_s[rows, :] + hg_ib_s[rows, :], hgn, D_A, rows)

        def ret_tile(i):
            rows = rows_of(i)
            p = _dot(xn_s[rows, :], w_in_ref[li, :, COL_C:COL_C + 4 * D_C])
            ycat_s[rows, D_A + D_B:D_MIX] = _silu(p[:, 3 * D_C:4 * D_C]).astype(BF16)
            for h in range(N_HEADS):
                hs = slice(h * HEAD_W, (h + 1) * HEAD_W)
                qh = p[:, hs]
                kh = p[:, D_C + h * HEAD_W:D_C + (h + 1) * HEAD_W]
                vh = p[:, 2 * D_C + h * HEAD_W:2 * D_C + (h + 1) * HEAD_W].astype(BF16)
                if latent:
                    cos = cos_ref[rows, :]
                    sa = sina_ref[rows, :]
                    sb = sinb_ref[rows, :]
                    qh = qh * cos + pltpu.roll(qh, HEAD_W - 32, 1) * sa + pltpu.roll(qh, 32, 1) * sb
                    kh = kh * cos + pltpu.roll(kh, HEAD_W - 32, 1) * sa + pltpu.roll(kh, 32, 1) * sb
                qh = qh * (RET_DK ** -0.5)
                qb = qh.astype(BF16)
                hq_s[rows, hs] = qb
                sc = _dot_nt(qb, kh.astype(BF16)) * ds_s[li, h]
                ret_s[rows, hs] = _dot(sc.astype(BF16), vh)
                for d in range(2):
                    kw = (kh * wtab_s[li, d, h]).astype(BF16)
                    kv_s[(i * 2 + d) * N_HEADS + h] = _dot_tn(kw, vh)

        def ret_carry():
            for d in range(2):
                for h in range(N_HEADS):
                    hs = slice(h * HEAD_W, (h + 1) * HEAD_W)
                    if latent:
                        state = s_ret_ref[li, d, h]
                    else:
                        state = jnp.zeros((HEAD_W, HEAD_W), F32)
                    g_c = gc_s[li, d, h][0:1, :]
                    order = range(n_tiles) if d == 0 else range(n_tiles - 1, -1, -1)
                    for n in order:
                        rows = pl.ds(n * TM, TM)
                        if ret_carried:
                            qw =(hq_s[rows, hs].astype(F32) * wtab_s[li, 2 + d, h]).astype(BF16)
                            ret_s[rows, hs] = ret_s[rows, hs] + _dot(qw, state.astype(BF16))
                        state = state * g_c + kv_s[(n * 2 + d) * N_HEADS + h]
                    if not latent:
                        f_ret_ref[layer, d, h] = state

        rtn = rtn_ref[li]

        def ret_out_tile(i):
            rows = rows_of(i)
            head_norm_gate(ret_s[rows, :], rtn, D_A + D_B, rows)

        fg = fg_ref[...]

        def out_tile(i):
            rows = rows_of(i)
            delta = _dot(ycat_s[rows, :], w_out_ref[li])
            out = src_ref[rows, :] + gate * delta
            if last:
                ms = jnp.mean(out * out, axis=-1, keepdims=True)
                out = out * lax.rsqrt(ms + EPS) * fg
            dst_ref[rows, :] = out

        zero_pad = jnp.zeros((8, D_A), F32)
        xpad_s[pl.ds(0, 8), :] = zero_pad
        xpad_s[pl.ds(seq + 8, 8), :] = zero_pad

        def redo_tile_if(unsafe, i):
            @pl.when(unsafe)
            def _():
                hg_tile_direct(i)

        if merged:
            def front_tile(i):
                norm_tile(i)
                lru_proj_tile(i)

            def mixer_tile(i):
                flag_s[i] = hg_tile(i).astype(jnp.int32)
                ret_tile(i)
                lru_gate_tile(i)

            def scan_chunk(j, carry):
                hg_scan(j)
                for g in range(HG_CHUNK // 8):
                    carry = lru_steps(j * (HG_CHUNK // 8) + g, carry)
                return carry

            def back_tile(i):
                lru_out_tile(i)
                hg_out_tile(i)
                ret_out_tile(i)
                out_tile(i)

            _loop(n_tiles, front_tile)
            _loop(n_tiles, mixer_tile)
            _loop(n_tiles, lambda i: redo_tile_if(flag_s[i] == 1, i))
            hg_state_init()
            lru_finish(lax.fori_loop(0, n_chunks, scan_chunk, (h0f, h0b)))
            hg_finish()
            ret_carry()
            _loop(n_tiles, back_tile)
        else:
            _loop(n_tiles, norm_tile)
            _loop(n_tiles, lru_proj_tile)
            _loop(n_tiles, lru_gate_tile)
            lru_finish(lax.fori_loop(0, seq // 8, lru_steps, (h0f, h0b)))
            _loop(n_tiles, lru_out_tile)
            _loop(n_tiles, lambda i: redo_tile_if(hg_tile(i), i))
            hg_state_init()
            _loop(n_chunks, hg_scan, unroll=2)
            hg_finish()
            _loop(n_tiles, hg_out_tile)
            _loop(n_tiles, ret_tile)
            ret_carry()
            _loop(n_tiles, ret_out_tile)
            _loop(n_tiles, out_tile)

    for li, layer in enumerate(layers):
        src = x_ref if li == 0 else xmid_s
        dst = y_ref if li == n_local - 1 else xmid_s
        run_layer(li, layer, src, dst)


def _trunk_call(x, mods, layers, latent, prm, states, rope):
    bsz, seq, _ = x.shape
    n_tiles = seq // TM
    n_chunks = seq // HG_CHUNK
    n_local = len(layers)
    l0 = layers[0]
    state_tail = (2, N_HEADS, HEAD_W, HEAD_W)

    def layers_spec(arr):
        tail = arr.shape[1:]
        zeros = (0,) * len(tail)
        return pl.BlockSpec((n_local,) + tail, lambda b: (l0 // n_local,) + zeros, pipeline_mode=pl.Buffered(1))

    def whole_spec(arr):
        zeros = (0,) * arr.ndim
        return pl.BlockSpec(arr.shape, lambda b: zeros, pipeline_mode=pl.Buffered(1))

    def per_seq_layers(tail, n_lay, first):
        zeros = (0,) * len(tail)
        return pl.BlockSpec((None, n_lay) + tail, lambda b: (b, first // n_lay) + zeros)

    act_mode = dict(pipeline_mode=pl.Buffered(1)) if latent else {}
    act_spec = pl.BlockSpec((None, seq, D_MODEL), lambda b: (b, 0, 0), **act_mode)
    cond_row = (lambda b: (l0 // n_local, 1 + b, 0, 0)) if latent else (lambda b: (l0 // n_local, 0, 0, 0))
    mod_spec = pl.BlockSpec((n_local, None, 3, D_MODEL), cond_row)

    stacked = [prm[k] for k in ("norm_g", "w_in", "conv_w", "conv_b", "wblk", "gbias", "lam")]
    operands = [x, mods] + stacked + [prm["lb_logits"], prm["hgrn_norm_g"], prm["rdl"], prm["ret_norm_g"],
                                      prm["w_out"], prm["final_g"]]
    in_specs = ([act_spec, mod_spec] + [layers_spec(a) for a in stacked]
                + [whole_spec(prm["lb_logits"]), layers_spec(prm["hgrn_norm_g"]), layers_spec(prm["rdl"]),
                   layers_spec(prm["ret_norm_g"]), layers_spec(prm["w_out"]), whole_spec(prm["final_g"])])
    if latent:
        operands += list(states) + list(rope)
        in_specs += [per_seq_layers((2, D_A), n_local, l0), per_seq_layers(state_tail, n_local, l0),
                     per_seq_layers(state_tail, n_local, l0)]
        in_specs += [whole_spec(r) for r in rope]
    out_shape = [jax.ShapeDtypeStruct((bsz, seq, D_MODEL), F32)]
    out_specs = [act_spec]
    if not latent:
        assert n_local == DEPTH, "the state outputs are written whole, one call must cover every layer"
        out_shape += [jax.ShapeDtypeStruct((bsz, DEPTH, 2, D_A), F32),
                      jax.ShapeDtypeStruct((bsz, DEPTH) + state_tail, F32),
                      jax.ShapeDtypeStruct((bsz, DEPTH) + state_tail, F32)]
        out_specs += [per_seq_layers((2, D_A), DEPTH, 0), per_seq_layers(state_tail, DEPTH, 0),
                      per_seq_layers(state_tail, DEPTH, 0)]
    scratch = [
        pltpu.VMEM((seq, D_MODEL), BF16),
        pltpu.VMEM((seq, D_MIX), BF16),
        pltpu.VMEM((seq + 16, D_A), F32),
        pltpu.VMEM((seq, D_A), F32), pltpu.VMEM((seq, D_A), F32),
        pltpu.VMEM((seq, D_A), F32), pltpu.VMEM((seq, D_A), F32),
        pltpu.VMEM((seq, D_B), BF16), pltpu.VMEM((seq, D_B), BF16),
        pltpu.VMEM((2, seq, D_B), BF16), pltpu.VMEM((2, seq, D_B), BF16),
        pltpu.VMEM((2 * n_chunks, D_B), F32),
        pltpu.VMEM((2 * N_HEADS, HEAD_W, HEAD_W), F32),
        pltpu.VMEM((n_local, N_HEADS, TM, TM), F32),
        pltpu.VMEM((n_local, 4 if (latent or n_tiles > 1) else 2, N_HEADS, TM, HEAD_W), F32),
        pltpu.VMEM((n_local, 2, N_HEADS, 8, HEAD_W), F32),
        pltpu.VMEM((n_tiles * 2 * N_HEADS, HEAD_W, HEAD_W), F32),
    ]
    if n_local > 1:
        scratch.append(pltpu.VMEM((seq, D_MODEL), F32))
    merged = not latent
    if merged:
        scratch += [pltpu.VMEM((seq, D_B), F32)] * 3
        scratch += [pltpu.VMEM((seq, D_C), F32)]
        scratch += [pltpu.SMEM((n_tiles,), jnp.int32)]
    body = functools.partial(_trunk_kernel, seq=seq, layers=tuple(layers), latent=latent, merged=merged)
    return pl.pallas_call(
        body,
        out_shape=out_shape,
        grid=(bsz,),
        in_specs=in_specs,
        out_specs=out_specs,
        scratch_shapes=scratch,
        compiler_params=pltpu.CompilerParams(dimension_semantics=("arbitrary",),
                                             vmem_limit_bytes=V7X_VMEM_LIMIT_BYTES),
        name=("latent" if latent else "context") + "_layers" + "".join(str(l) for l in layers),
    )(*operands)


def _rope_tables(seq):
    half = RET_DK // 2
    nf = half // 2
    t = np.arange(seq)
    freq = (ROPE_BASE ** (-np.arange(nf, dtype=np.float32) / nf)).astype(np.float32)
    ang_r = (t // GRID_W).astype(np.float32)[:, None] * freq[None, :]
    ang_c = (t % GRID_W).astype(np.float32)[:, None] * freq[None, :]
    zero = np.zeros((seq, nf), np.float32)
    cos = np.concatenate([np.cos(ang_r)] * 2 + [np.cos(ang_c)] * 2, axis=-1)
    sin_a = np.concatenate([-np.sin(ang_r), zero, -np.sin(ang_c), zero], axis=-1)
    sin_b = np.concatenate([zero, np.sin(ang_r), zero, np.sin(ang_c)], axis=-1)
    return tuple(jnp.asarray(a, F32) for a in (cos, sin_a, sin_b))


def _block_diag_gates(wa, wx):
    per_half = GATE_HALF // LRU_BW
    n_half = D_A // GATE_HALF
    eye = jnp.eye(per_half, dtype=F32)
    cols = []
    for d in range(2):
        for w in (wa, wx):
            blk = w[:, d].reshape(DEPTH, n_half, per_half, LRU_BW, LRU_BW)
            dense = jnp.einsum('pq,lhpij->lhpiqj', eye, blk).reshape(DEPTH, n_half, GATE_HALF, GATE_HALF)
            cols.append(dense)
    return jnp.concatenate(cols, axis=-1).astype(BF16)


def kernel(x_prompt, x_sample, state_lru, state_hgrn, state_ret, c, c_ctx, norm_g, w_mod, b_mod, w_in, conv_w,
           conv_b, lru_wa, lru_ba, lru_wx, lru_bx, lru_lambda, hgrn_lb_logits, hgrn_norm_g, ret_decay_logit,
           ret_norm_g, w_out, final_g):
    dec_b = x_sample.shape[0]
    pad = jnp.zeros((COND_ROWS - 1 - dec_b, D_MODEL), F32)
    cond = jnp.concatenate([c_ctx[None, :], c, pad], axis=0)
    mods = _modulation(cond, w_mod, b_mod).reshape(DEPTH, COND_ROWS, 3, D_MODEL)

    prm = dict(
        norm_g=norm_g[:, None, :],
        w_in=w_in.astype(BF16),
        conv_w=conv_w,
        conv_b=conv_b[:, None, :],
        wblk=_block_diag_gates(lru_wa, lru_wx),
        gbias=jnp.stack([lru_ba[:, 0], lru_bx[:, 0], lru_ba[:, 1], lru_bx[:, 1]], axis=1),
        lam=lru_lambda,
        lb_logits=hgrn_lb_logits.reshape(DEPTH * 2, D_B),
        hgrn_norm_g=hgrn_norm_g[:, None, :],
        rdl=jnp.broadcast_to(ret_decay_logit.reshape(DEPTH, 2 * RET_HEADS, 1), (DEPTH, 2 * RET_HEADS, TM)),
        ret_norm_g=ret_norm_g[:, None, :],
        w_out=w_out.astype(BF16),
        final_g=final_g[None, :],
    )

    y_prompt, new_lru, new_hgrn, new_ret = _trunk_call(x_prompt, mods, list(range(DEPTH)), False, prm, None, None)

    rope = _rope_tables(x_sample.shape[1])
    z = x_sample
    for l in range(DEPTH):
        (z,) = _trunk_call(z, mods, [l], True, prm, (state_lru, state_hgrn, state_ret), rope)
    y_sample = z

    return (y_prompt, y_sample, new_lru, new_hgrn, new_ret)
```

```python
import functools

import numpy as np

import jax
import jax.numpy as jnp
from jax import lax
from jax.experimental import pallas as pl
from jax.experimental.pallas import tpu as pltpu

F32 = jnp.float32
BF16 = jnp.bfloat16

D_MODEL = 1024
DEPTH = 2
GRID_W = 64
D_A = 512
LRU_BLOCKS = 8
LRU_BW = D_A // LRU_BLOCKS
CONV_W = 4
CONV_LEFT = 1
LRU_C = 8.0
D_B = 512
HG_HEADS = 4
HG_DK = D_B // HG_HEADS
HG_CHUNK = 32
F_MIN = 1e-20
D_C = 512
RET_HEADS = 4
RET_DK = D_C // RET_HEADS
ROPE_BASE = 10000.0
D_MIX = D_A + D_B + D_C
D_IN = 2 * D_A + 5 * D_B + 4 * D_C
EPS = 1e-6

COL_A = 0
COL_B = 2 * D_A
COL_C = COL_B + 5 * D_B

TM = 256
HEAD_W = 128
N_HEADS = 4
GATE_HALF = 256
SAFE_LOG_DECAY = 70.0
V7X_VMEM_LIMIT_BYTES = 60000 * 1024
COND_ROWS = 8


def _sigmoid(x):
    return 0.5 * jnp.tanh(0.5 * x) + 0.5


def _silu(x):
    h = 0.5 * x
    return h * jnp.tanh(h) + h


def _dot(a, b):
    return jnp.dot(a, b, preferred_element_type=F32)


def _dot_nt(a, b):
    return lax.dot_general(a, b, (((1,), (1,)), ((), ())), preferred_element_type=F32)


def _dot_tn(a, b):
    return lax.dot_general(a, b, (((0,), (0,)), ((), ())), preferred_element_type=F32)


def _loop(n, body, unroll=1):
    if n == 1:
        body(0)
    else:
        def wrapped(i, carry):
            body(i)
            return carry
        lax.fori_loop(0, n, wrapped, 0, unroll=unroll)


def _mod_kernel(c_ref, w_ref, b_ref, o_ref):
    c = c_ref[...]
    o_ref[...] = _dot(_silu(c), w_ref[...]) + b_ref[...]


def _modulation(cond, w_mod, b_mod):
    n_col = 3 * D_MODEL
    blk = D_MODEL
    return pl.pallas_call(
        _mod_kernel,
        out_shape=jax.ShapeDtypeStruct((DEPTH, COND_ROWS, n_col), F32),
        grid=(DEPTH, n_col // blk),
        in_specs=[
            pl.BlockSpec((COND_ROWS, D_MODEL), lambda l, j: (0, 0)),
            pl.BlockSpec((None, D_MODEL, blk), lambda l, j: (l, 0, j)),
            pl.BlockSpec((None, 1, blk), lambda l, j: (l, 0, j)),
        ],
        out_specs=pl.BlockSpec((None, COND_ROWS, blk), lambda l, j: (l, 0, j)),
        name="modulation",
    )(cond, w_mod, b_mod.reshape(DEPTH, 1, n_col))


def _trunk_kernel(*refs, seq, layers, latent, merged):
    n_tiles = seq // TM
    n_chunks = seq // HG_CHUNK
    n_local = len(layers)
    ret_carried = latent or n_tiles > 1
    it = iter(refs)
    x_ref = next(it); mod_ref = next(it); ng_ref = next(it); w_in_ref = next(it)
    cw_ref = next(it); cb_ref = next(it); wblk_ref = next(it); gb_ref = next(it); lam_ref = next(it)
    lbl_ref = next(it); hgn_ref = next(it); rdl_ref = next(it); rtn_ref = next(it)
    w_out_ref = next(it); fg_ref = next(it)
    if latent:
        s_lru_ref = next(it); s_hg_ref = next(it); s_ret_ref = next(it)
        cos_ref = next(it); sina_ref = next(it); sinb_ref = next(it)
    y_ref = next(it)
    if not latent:
        f_lru_ref = next(it); f_hg_ref = next(it); f_ret_ref = next(it)
    xn_s = next(it); ycat_s = next(it); xpad_s = next(it)
    fa_s = next(it); fb_s = next(it); fc_s = next(it); fd_s = next(it)
    hv_s = next(it); hq_s = next(it); hqb_s = next(it); hkd_s = next(it)
    ebl_s = next(it); st_s = next(it)
    ds_s = next(it); wtab_s = next(it); gc_s = next(it); kv_s = next(it)
    xmid_s = next(it) if n_local > 1 else None
    if merged:
        hg_intra_s = next(it); hg_if_s = next(it); hg_ib_s = next(it); ret_s = next(it)
        flag_s = next(it)
        tmp_b_s, tmp_q_s, tmp_k_s = hg_if_s, hg_ib_s, xpad_s
        lru_hf_s, lru_hb_s = fa_s, fc_s
    else:
        hg_intra_s, hg_if_s, hg_ib_s = fc_s, fa_s, fb_s
        tmp_b_s, tmp_q_s, tmp_k_s = fa_s, fb_s, fd_s
        lru_hf_s, lru_hb_s, ret_s = xpad_s, fa_s, xpad_s

    def rows_of(i):
        return pl.ds(pl.multiple_of(i * TM, TM), TM)

    @pl.when(pl.program_id(0) == 0)
    def _():
        ti = lax.broadcasted_iota(jnp.int32, (TM, TM), 0)
        si = lax.broadcasted_iota(jnp.int32, (TM, TM), 1)
        rel = (ti - si).astype(F32)
        tcol = lax.broadcasted_iota(jnp.int32, (TM, HEAD_W), 0).astype(F32)
        for li in range(n_local):
            lg_all = -jax.nn.softplus(-rdl_ref[li])
            for h in range(N_HEADS):
                lg_f = lg_all[h:h + 1, :]
                lg_b = lg_all[N_HEADS + h:N_HEADS + h + 1, :]
                d_f = jnp.where(ti >= si, jnp.exp(jnp.where(ti >= si, rel, 0.0) * lg_f), 0.0)
                d_b = jnp.where(si >= ti, jnp.exp(jnp.where(si >= ti, -rel, 0.0) * lg_b), 0.0)
                ds_s[li, h] = d_f + d_b
                lgf = lg_f[:, :HEAD_W]
                lgb = lg_b[:, :HEAD_W]
                wtab_s[li, 0, h] = jnp.exp((TM - 1.0 - tcol) * lgf)
                wtab_s[li, 1, h] = jnp.exp(tcol * lgb)
                if ret_carried:
                    wtab_s[li, 2, h] = jnp.exp((tcol + 1.0) * lgf)
                    wtab_s[li, 3, h] = jnp.exp((TM - tcol) * lgb)
                gc_s[li, 0, h] = jnp.broadcast_to(jnp.exp(TM * lgf), (8, HEAD_W))
                gc_s[li, 1, h] = jnp.broadcast_to(jnp.exp(TM * lgb), (8, HEAD_W))

    ri = lax.broadcasted_iota(jnp.int32, (TM, TM), 0)
    ci = lax.broadcasted_iota(jnp.int32, (TM, TM), 1)
    same_chunk = (ri // HG_CHUNK) == (ci // HG_CHUNK)
    keeps = (same_chunk & (ci <= ri), same_chunk & (ci >= ri))
    cum_mats = tuple(jnp.where(kp, 1.0, 0.0).astype(BF16) for kp in keeps)
    cpt = TM // HG_CHUNK
    rrow = lax.broadcasted_iota(jnp.int32, (HG_CHUNK, HEAD_W), 0)
    chunk_shape = (cpt, HG_CHUNK, D_B)
    last_row = (HG_CHUNK - 1, 0)
    mid_row = (HG_CHUNK // 2 - 1, HG_CHUNK // 2)
    lbl = lbl_ref[...]

    def run_layer(li, layer, src_ref, dst_ref):
        last = layer == DEPTH - 1
        shift = mod_ref[li, 0:1, :]
        scale = mod_ref[li, 1:2, :]
        gate = mod_ref[li, 2:3, :]

        ng = ng_ref[li]

        def norm_tile(i):
            rows = rows_of(i)
            x = src_ref[rows, :]
            ms = jnp.mean(x * x, axis=-1, keepdims=True)
            xn = (x * lax.rsqrt(ms + EPS)) * ng
            xn = xn * (1.0 + scale) + shift
            xn_s[rows, :] = xn.astype(BF16)

        def lru_proj_tile(i):
            rows = rows_of(i)
            pa = _dot(xn_s[rows, :], w_in_ref[li, :, COL_A:COL_A + 2 * D_A])
            xpad_s[pl.ds(pl.multiple_of(i * TM + 8, 8), TM), :] = pa[:, :D_A]
            ycat_s[rows, 0:D_A] = _silu(pa[:, D_A:]).astype(BF16)

        cw = cw_ref[li]
        cb = cb_ref[li]
        nsp = jax.nn.softplus(-lam_ref[li])
        a_refs = (fa_s, fc_s)
        b_refs = (fb_s, fd_s)

        def lru_gate_tile(i):
            rows = rows_of(i)
            win = xpad_s[pl.ds(pl.multiple_of(i * TM, TM), TM + 16), :]
            wn = TM + 16
            u = cb
            for k in range(CONV_W):
                off = k - CONV_LEFT
                sh = win if off == 0 else pltpu.roll(win, (-off) % wn, 0)
                u = u + sh[8:8 + TM, :] * cw[k:k + 1, :]
            for hf in range(D_A // GATE_HALF):
                cols = slice(hf * GATE_HALF, (hf + 1) * GATE_HALF)
                uh = u[:, cols]
                g4 = _dot(uh.astype(BF16), wblk_ref[li, hf])
                for d in range(2):
                    c0 = 2 * d * GATE_HALF
                    r = _sigmoid(g4[:, c0:c0 + GATE_HALF] + gb_ref[li, 2 * d:2 * d + 1, cols])
                    ig = _sigmoid(g4[:, c0 + GATE_HALF:c0 + 2 * GATE_HALF] + gb_ref[li, 2 * d + 1:2 * d + 2, cols])
                    h = jnp.tanh((-0.5 * LRU_C) * r * nsp[d:d + 1, cols])
                    inv = 1.0 / (1.0 - h)
                    nh = -h
                    root = jnp.where(nh > 0.0, nh * lax.rsqrt(nh), 0.0)
                    a_refs[d][rows, cols] = (1.0 + h) * inv
                    b_refs[d][rows, cols] = (2.0 * root * inv) * (ig * uh)

        if latent:
            h0f = s_lru_ref[li, 0:1, :]
            h0b = s_lru_ref[li, 1:2, :]
        else:
            h0f = jnp.zeros((1, D_A), F32)
            h0b = h0f

        def lru_fwd_steps(g, hf):
            tf = pl.multiple_of(g * 8, 8)
            for k in range(8):
                rf = pl.ds(tf + k, 1)
                hf = fa_s[rf, :] * hf + fb_s[rf, :]
                lru_hf_s[rf, :] = hf
            return hf

        def lru_bwd_steps(g, hb):
            tb = pl.multiple_of(seq - 8 - g * 8, 8)
            for k in range(7, -1, -1):
                rb = pl.ds(tb + k, 1)
                hb = fc_s[rb, :] * hb + fd_s[rb, :]
                lru_hb_s[rb, :] = hb
            return hb

        def lru_steps(g, carry):
            return lru_fwd_steps(g, carry[0]), lru_bwd_steps(g, carry[1])

        def lru_finish(carry):
            if not latent:
                f_lru_ref[layer, 0:1, :] = carry[0]
                f_lru_ref[layer, 1:2, :] = carry[1]

        def lru_out_tile(i):
            rows = rows_of(i)
            ya = (lru_hf_s[rows, :] + lru_hb_s[rows, :]) * ycat_s[rows, 0:D_A].astype(F32)
            ycat_s[rows, 0:D_A] = ya.astype(BF16)

        lb_rows = []
        for d in range(2):
            ls = [lbl[k * 2 + d:k * 2 + d + 1, :] for k in range(DEPTH)]
            mx = functools.reduce(jnp.maximum, ls)
            es = [jnp.exp(v - mx) for v in ls]
            den = functools.reduce(lambda p, q: p + q, es)
            acc = jnp.zeros_like(mx)
            for k in range(1, layer + 1):
                acc = acc + es[k] / den
            lb_rows.append(acc)

        def hg_proj(i):
            return _dot(xn_s[rows_of(i), :], w_in_ref[li, :, COL_B:COL_B + 5 * D_B])

        def hg_gates(p, d):
            z = p[:, (1 + d) * D_B:(2 + d) * D_B]
            lb = lb_rows[d]
            sig = _sigmoid(z)
            f = lb + (1.0 - lb) * sig
            logf = jnp.log(jnp.maximum(f, F_MIN))
            k = (1.0 - lb) * (1.0 - sig)
            hi = logf.astype(BF16)
            lo = (logf - hi.astype(F32)).astype(BF16)
            b = _dot(cum_mats[d], hi) + _dot(cum_mats[d], lo)
            return k.reshape(chunk_shape), b.reshape(chunk_shape)

        def hg_tile(i):
            rows = rows_of(i)
            p = hg_proj(i)
            q3 = _silu(p[:, 0:D_B]).reshape(chunk_shape)
            v = p[:, 3 * D_B:4 * D_B].astype(BF16)
            hv_s[rows, :] = v
            ycat_s[rows, D_A:D_A + D_B] = _silu(p[:, 4 * D_B:5 * D_B]).astype(BF16)
            qmb, kib = [], []
            worst = None
            for d in range(2):
                k3, b3 = hg_gates(p, d)
                bl3 = b3[:, last_row[d]:last_row[d] + 1, :]
                m3 = b3[:, mid_row[d]:mid_row[d] + 1, :]
                rel = b3 - m3
                qm = q3 * jnp.exp(rel)
                ki = k3 * jnp.exp(-rel)
                hqb_s[d, rows, :] = (qm * jnp.exp(m3)).reshape(TM, D_B).astype(BF16)
                hkd_s[d, rows, :] = (ki * jnp.exp(bl3 - m3)).reshape(TM, D_B).astype(BF16)
                ebl_s[pl.ds(pl.multiple_of(d * n_chunks + i * cpt, cpt), cpt), :] = jnp.exp(bl3.reshape(cpt, D_B))
                span = jnp.maximum(jnp.max(-m3), jnp.max(m3 - bl3))
                worst = span if worst is None else jnp.maximum(worst, span)
                qmb.append(qm.reshape(TM, D_B).astype(BF16))
                kib.append(ki.reshape(TM, D_B).astype(BF16))
            for h in range(N_HEADS):
                hs = slice(h * HEAD_W, (h + 1) * HEAD_W)
                att = (jnp.where(keeps[0], _dot_nt(qmb[0][:, hs], kib[0][:, hs]), 0.0)
                       + jnp.where(keeps[1], _dot_nt(qmb[1][:, hs], kib[1][:, hs]), 0.0))
                hg_intra_s[rows, hs] = _dot(att.astype(BF16), v[:, hs])
            return worst > SAFE_LOG_DECAY

        def hg_tile_direct(i):
            rows = rows_of(i)
            p = hg_proj(i)
            q3 = _silu(p[:, 0:D_B]).reshape(chunk_shape)
            tmp_q_s[rows, :] = q3.reshape(TM, D_B)
            for d in range(2):
                k3, b3 = hg_gates(p, d)
                bl3 = b3[:, last_row[d]:last_row[d] + 1, :]
                hqb_s[d, rows, :] = (q3 * jnp.exp(b3)).reshape(TM, D_B).astype(BF16)
                hkd_s[d, rows, :] = (k3 * jnp.exp(bl3 - b3)).reshape(TM, D_B).astype(BF16)
                tmp_b_s[rows, :] = b3.reshape(TM, D_B)
                tmp_k_s[rows, :] = k3.reshape(TM, D_B)

                def chunk(c):
                    rws = pl.ds(pl.multiple_of(i * TM + c * HG_CHUNK, HG_CHUNK), HG_CHUNK)
                    for h in range(N_HEADS):
                        hs = slice(h * HEAD_W, (h + 1) * HEAD_W)
                        bf = tmp_b_s[rws, hs]
                        qf = tmp_q_s[rws, hs]
                        kf = tmp_k_s[rws, hs]
                        vf = hv_s[rws, hs].astype(F32)

                        def src(s, acc):
                            pick = rrow == s
                            ks = jnp.sum(jnp.where(pick, kf, 0.0), axis=0, keepdims=True)
                            vs = jnp.sum(jnp.where(pick, vf, 0.0), axis=0, keepdims=True)
                            bs = jnp.sum(jnp.where(pick, bf, 0.0), axis=0, keepdims=True)
                            w = jnp.sum(qf * ks * jnp.exp(jnp.minimum(bf - bs, 0.0)), axis=-1, keepdims=True)
                            keep = (rrow[:, 0:1] >= s) if d == 0 else (rrow[:, 0:1] <= s)
                            return acc + jnp.where(keep, w, 0.0) * vs

                        o = lax.fori_loop(0, HG_CHUNK, src, jnp.zeros((HG_CHUNK, HEAD_W), F32))
                        hg_intra_s[rws, hs] = o if d == 0 else hg_intra_s[rws, hs] + o

                _loop(cpt, chunk)

        def hg_state_init():
            for d in range(2):
                for h in range(N_HEADS):
                    if latent:
                        st_s[d * N_HEADS + h] = s_hg_ref[li, d, h].T
                    else:
                        st_s[d * N_HEADS + h] = jnp.zeros((HEAD_W, HEAD_W), F32)

        inter_refs = (hg_if_s, hg_ib_s)

        def hg_scan(j):
            for d in range(2):
                c = j if d == 0 else n_chunks - 1 - j
                rws = pl.ds(pl.multiple_of(c * HG_CHUNK, HG_CHUNK), HG_CHUNK)
                ebl = ebl_s[pl.ds(d * n_chunks + c, 1), :]
                for h in range(N_HEADS):
                    hs = slice(h * HEAD_W, (h + 1) * HEAD_W)
                    st = st_s[d * N_HEADS + h]
                    inter_refs[d][rws, hs] = _dot_nt(hqb_s[d, rws, hs], st.astype(BF16))
                    st_s[d * N_HEADS + h] = st * ebl[:, hs] + _dot_tn(hv_s[rws, hs], hkd_s[d, rws, hs])

        def hg_finish():
            if not latent:
                for d in range(2):
                    for h in range(N_HEADS):
                        f_hg_ref[layer, d, h] = st_s[d * N_HEADS + h].T

        def head_norm_gate(y, gain, col0, rows):
            for h in range(N_HEADS):
                hs = slice(h * HEAD_W, (h + 1) * HEAD_W)
                yh = y[:, hs]
                ms = jnp.mean(yh * yh, axis=-1, keepdims=True)
                yn = yh * lax.rsqrt(ms + EPS) * gain[:, hs]
                cs = slice(col0 + h * HEAD_W, col0 + (h + 1) * HEAD_W)
                ycat_s[rows, cs] = (yn * ycat_s[rows, cs].astype(F32)).astype(BF16)

        hgn = hgn_ref[li]

        def hg_out_tile(i):
            rows = rows_of(i)
            head_norm_gate(hg_intra_s[rows, :] + hg_if_s[rows, :] + hg_ib_s[rows, :], hgn, D_A, rows)

        def ret_tile(i):
            rows = rows_of(i)
            p = _dot(xn_s[rows, :], w_in_ref[li, :, COL_C:COL_C + 4 * D_C])
            ycat_s[rows, D_A + D_B:D_MIX] = _silu(p[:, 3 * D_C:4 * D_C]).astype(BF16)
            for h in range(N_HEADS):
                hs = slice(h * HEAD_W, (h + 1) * HEAD_W)
                qh = p[:, hs]
                kh = p[:, D_C + h * HEAD_W:D_C + (h + 1) * HEAD_W]
                vh = p[:, 2 * D_C + h * HEAD_W:2 * D_C + (h + 1) * HEAD_W].astype(BF16)
                if latent:
                    cos = cos_ref[rows, :]
                    sa = sina_ref[rows, :]
                    sb = sinb_ref[rows, :]
                    qh = qh * cos + pltpu.roll(qh, HEAD_W - 32, 1) * sa + pltpu.roll(qh, 32, 1) * sb
                    kh = kh * cos + pltpu.roll(kh, HEAD_W - 32, 1) * sa + pltpu.roll(kh, 32, 1) * sb
                qh = qh * (RET_DK ** -0.5)
                qb = qh.astype(BF16)
                hq_s[rows, hs] = qb
                sc = _dot_nt(qb, kh.astype(BF16)) * ds_s[li, h]
                ret_s[rows, hs] = _dot(sc.astype(BF16), vh)
                for d in range(2):
                    kw = (kh * wtab_s[li, d, h]).astype(BF16)
                    kv_s[(i * 2 + d) * N_HEADS + h] = _dot_tn(kw, vh)

        def ret_carry():
            for d in range(2):
                for h in range(N_HEADS):
                    hs = slice(h * HEAD_W, (h + 1) * HEAD_W)
                    if latent:
                        state = s_ret_ref[li, d, h]
                    else:
                        state = jnp.zeros((HEAD_W, HEAD_W), F32)
                    g_c = gc_s[li, d, h][0:1, :]
                    order = range(n_tiles) if d == 0 else range(n_tiles - 1, -1, -1)
                    for n in order:
                        rows = pl.ds(n * TM, TM)
                        if ret_carried:
                            qw =(hq_s[rows, hs].astype(F32) * wtab_s[li, 2 + d, h]).astype(BF16)
                            ret_s[rows, hs] = ret_s[rows, hs] + _dot(qw, state.astype(BF16))
                        state = state * g_c + kv_s[(n * 2 + d) * N_HEADS + h]
                    if not latent:
                        f_ret_ref[layer, d, h] = state

        rtn = rtn_ref[li]

        def ret_out_tile(i):
            rows = rows_of(i)
            head_norm_gate(ret_s[rows, :], rtn, D_A + D_B, rows)

        fg = fg_ref[...]

        def out_tile(i):
            rows = rows_of(i)
            delta = _dot(ycat_s[rows, :], w_out_ref[li])
            out = src_ref[rows, :] + gate * delta
            if last:
                ms = jnp.mean(out * out, axis=-1, keepdims=True)
                out = out * lax.rsqrt(ms + EPS) * fg
            dst_ref[rows, :] = out

        zero_pad = jnp.zeros((8, D_A), F32)
        xpad_s[pl.ds(0, 8), :] = zero_pad
        xpad_s[pl.ds(seq + 8, 8), :] = zero_pad

        def redo_tile_if(unsafe, i):
            @pl.when(unsafe)
            def _():
                hg_tile_direct(i)

        if merged:
            def front_tile(i):
                norm_tile(i)
                lru_proj_tile(i)

            def mixer_tile(i):
                flag_s[i] = hg_tile(i).astype(jnp.int32)
                ret_tile(i)
                lru_gate_tile(i)

            def scan_chunk(j, carry):
                hg_scan(j)
                for g in range(HG_CHUNK // 8):
                    carry = lru_steps(j * (HG_CHUNK // 8) + g, carry)
                return carry

            def back_tile(i):
                lru_out_tile(i)
                hg_out_tile(i)
                ret_out_tile(i)
                out_tile(i)

            _loop(n_tiles, front_tile)
            _loop(n_tiles, mixer_tile)
            _loop(n_tiles, lambda i: redo_tile_if(flag_s[i] == 1, i))
            hg_state_init()
            lru_finish(lax.fori_loop(0, n_chunks, scan_chunk, (h0f, h0b), unroll=8))
            hg_finish()
            ret_carry()
            _loop(n_tiles, back_tile)
        else:
            def front_tile(i):
                norm_tile(i)
                lru_proj_tile(i)

            def hg_ret_tile(i):
                unsafe = hg_tile(i)
                ret_tile(i)
                redo_tile_if(unsafe, i)

            def back_tile(i):
                hg_out_tile(i)
                ret_out_tile(i)
                out_tile(i)

            _loop(n_tiles, front_tile)
            _loop(n_tiles, lru_gate_tile)
            hf_fin = lax.fori_loop(0, seq // 8, lru_fwd_steps, h0f)
            hb_fin = lax.fori_loop(0, seq // 8, lru_bwd_steps, h0b)
            lru_finish((hf_fin, hb_fin))
            _loop(n_tiles, lru_out_tile)
            _loop(n_tiles, hg_ret_tile)
            hg_state_init()
            _loop(n_chunks, hg_scan, unroll=8)
            hg_finish()
            ret_carry()
            _loop(n_tiles, back_tile)

    for li, layer in enumerate(layers):
        src = x_ref if li == 0 else xmid_s
        dst = y_ref if li == n_local - 1 else xmid_s
        run_layer(li, layer, src, dst)


def _trunk_call(x, mods, layers, latent, prm, states, rope):
    bsz, seq, _ = x.shape
    n_tiles = seq // TM
    n_chunks = seq // HG_CHUNK
    n_local = len(layers)
    l0 = layers[0]
    state_tail = (2, N_HEADS, HEAD_W, HEAD_W)

    def layers_spec(arr):
        tail = arr.shape[1:]
        zeros = (0,) * len(tail)
        return pl.BlockSpec((n_local,) + tail, lambda b: (l0 // n_local,) + zeros, pipeline_mode=pl.Buffered(1))

    def whole_spec(arr):
        zeros = (0,) * arr.ndim
        return pl.BlockSpec(arr.shape, lambda b: zeros, pipeline_mode=pl.Buffered(1))

    def per_seq_layers(tail, n_lay, first):
        zeros = (0,) * len(tail)
        return pl.BlockSpec((None, n_lay) + tail, lambda b: (b, first // n_lay) + zeros)

    act_mode = dict(pipeline_mode=pl.Buffered(1)) if latent else {}
    act_spec = pl.BlockSpec((None, seq, D_MODEL), lambda b: (b, 0, 0), **act_mode)
    cond_row = (lambda b: (l0 // n_local, 1 + b, 0, 0)) if latent else (lambda b: (l0 // n_local, 0, 0, 0))
    mod_spec = pl.BlockSpec((n_local, None, 3, D_MODEL), cond_row)

    stacked = [prm[k] for k in ("norm_g", "w_in", "conv_w", "conv_b", "wblk", "gbias", "lam")]
    operands = [x, mods] + stacked + [prm["lb_logits"], prm["hgrn_norm_g"], prm["rdl"], prm["ret_norm_g"],
                                      prm["w_out"], prm["final_g"]]
    in_specs = ([act_spec, mod_spec] + [layers_spec(a) for a in stacked]
                + [whole_spec(prm["lb_logits"]), layers_spec(prm["hgrn_norm_g"]), layers_spec(prm["rdl"]),
                   layers_spec(prm["ret_norm_g"]), layers_spec(prm["w_out"]), whole_spec(prm["final_g"])])
    if latent:
        operands += list(states) + list(rope)
        in_specs += [per_seq_layers((2, D_A), n_local, l0), per_seq_layers(state_tail, n_local, l0),
                     per_seq_layers(state_tail, n_local, l0)]
        in_specs += [whole_spec(r) for r in rope]
    out_shape = [jax.ShapeDtypeStruct((bsz, seq, D_MODEL), F32)]
    out_specs = [act_spec]
    if not latent:
        assert n_local == DEPTH, "the state outputs are written whole, one call must cover every layer"
        out_shape += [jax.ShapeDtypeStruct((bsz, DEPTH, 2, D_A), F32),
                      jax.ShapeDtypeStruct((bsz, DEPTH) + state_tail, F32),
                      jax.ShapeDtypeStruct((bsz, DEPTH) + state_tail, F32)]
        out_specs += [per_seq_layers((2, D_A), DEPTH, 0), per_seq_layers(state_tail, DEPTH, 0),
                      per_seq_layers(state_tail, DEPTH, 0)]
    scratch = [
        pltpu.VMEM((seq, D_MODEL), BF16),
        pltpu.VMEM((seq, D_MIX), BF16),
        pltpu.VMEM((seq + 16, D_A), F32),
        pltpu.VMEM((seq, D_A), F32), pltpu.VMEM((seq, D_A), F32),
        pltpu.VMEM((seq, D_A), F32), pltpu.VMEM((seq, D_A), F32),
        pltpu.VMEM((seq, D_B), BF16), pltpu.VMEM((seq, D_B), BF16),
        pltpu.VMEM((2, seq, D_B), BF16), pltpu.VMEM((2, seq, D_B), BF16),
        pltpu.VMEM((2 * n_chunks, D_B), F32),
        pltpu.VMEM((2 * N_HEADS, HEAD_W, HEAD_W), F32),
        pltpu.VMEM((n_local, N_HEADS, TM, TM), F32),
        pltpu.VMEM((n_local, 4 if (latent or n_tiles > 1) else 2, N_HEADS, TM, HEAD_W), F32),
        pltpu.VMEM((n_local, 2, N_HEADS, 8, HEAD_W), F32),
        pltpu.VMEM((n_tiles * 2 * N_HEADS, HEAD_W, HEAD_W), F32),
    ]
    if n_local > 1:
        scratch.append(pltpu.VMEM((seq, D_MODEL), F32))
    merged = not latent
    if merged:
        scratch += [pltpu.VMEM((seq, D_B), F32)] * 3
        scratch += [pltpu.VMEM((seq, D_C), F32)]
        scratch += [pltpu.SMEM((n_tiles,), jnp.int32)]
    body = functools.partial(_trunk_kernel, seq=seq, layers=tuple(layers), latent=latent, merged=merged)
    return pl.pallas_call(
        body,
        out_shape=out_shape,
        grid=(bsz,),
        in_specs=in_specs,
        out_specs=out_specs,
        scratch_shapes=scratch,
        compiler_params=pltpu.CompilerParams(dimension_semantics=("arbitrary",),
                                             vmem_limit_bytes=V7X_VMEM_LIMIT_BYTES),
        name=("latent" if latent else "context") + "_layers" + "".join(str(l) for l in layers),
    )(*operands)


def _rope_tables(seq):
    half = RET_DK // 2
    nf = half // 2
    t = np.arange(seq)
    freq = (ROPE_BASE ** (-np.arange(nf, dtype=np.float32) / nf)).astype(np.float32)
    ang_r = (t // GRID_W).astype(np.float32)[:, None] * freq[None, :]
    ang_c = (t % GRID_W).astype(np.float32)[:, None] * freq[None, :]
    zero = np.zeros((seq, nf), np.float32)
    cos = np.concatenate([np.cos(ang_r)] * 2 + [np.cos(ang_c)] * 2, axis=-1)
    sin_a = np.concatenate([-np.sin(ang_r), zero, -np.sin(ang_c), zero], axis=-1)
    sin_b = np.concatenate([zero, np.sin(ang_r), zero, np.sin(ang_c)], axis=-1)
    return tuple(jnp.asarray(a, F32) for a in (cos, sin_a, sin_b))


def _block_diag_gates(wa, wx):
    per_half = GATE_HALF // LRU_BW
    n_half = D_A // GATE_HALF
    eye = jnp.eye(per_half, dtype=F32)
    cols = []
    for d in range(2):
        for w in (wa, wx):
            blk = w[:, d].reshape(DEPTH, n_half, per_half, LRU_BW, LRU_BW)
            dense = jnp.einsum('pq,lhpij->lhpiqj', eye, blk).reshape(DEPTH, n_half, GATE_HALF, GATE_HALF)
            cols.append(dense)
    return jnp.concatenate(cols, axis=-1).astype(BF16)


def kernel(x_prompt, x_sample, state_lru, state_hgrn, state_ret, c, c_ctx, norm_g, w_mod, b_mod, w_in, conv_w,
           conv_b, lru_wa, lru_ba, lru_wx, lru_bx, lru_lambda, hgrn_lb_logits, hgrn_norm_g, ret_decay_logit,
           ret_norm_g, w_out, final_g):
    dec_b = x_sample.shape[0]
    pad = jnp.zeros((COND_ROWS - 1 - dec_b, D_MODEL), F32)
    cond = jnp.concatenate([c_ctx[None, :], c, pad], axis=0)
    mods = _modulation(cond, w_mod, b_mod).reshape(DEPTH, COND_ROWS, 3, D_MODEL)

    prm = dict(
        norm_g=norm_g[:, None, :],
        w_in=w_in.astype(BF16),
        conv_w=conv_w,
        conv_b=conv_b[:, None, :],
        wblk=_block_diag_gates(lru_wa, lru_wx),
        gbias=jnp.stack([lru_ba[:, 0], lru_bx[:, 0], lru_ba[:, 1], lru_bx[:, 1]], axis=1),
        lam=lru_lambda,
        lb_logits=hgrn_lb_logits.reshape(DEPTH * 2, D_B),
        hgrn_norm_g=hgrn_norm_g[:, None, :],
        rdl=jnp.broadcast_to(ret_decay_logit.reshape(DEPTH, 2 * RET_HEADS, 1), (DEPTH, 2 * RET_HEADS, TM)),
        ret_norm_g=ret_norm_g[:, None, :],
        w_out=w_out.astype(BF16),
        final_g=final_g[None, :],
    )

    y_prompt, new_lru, new_hgrn, new_ret = _trunk_call(x_prompt, mods, list(range(DEPTH)), False, prm, None, None)

    rope = _rope_tables(x_sample.shape[1])
    z = x_sample
    for l in range(DEPTH):
        (z,) = _trunk_call(z, mods, [l], True, prm, (state_lru, state_hgrn, state_ret), rope)
    y_sample = z

    return (y_prompt, y_sample, new_lru, new_hgrn, new_ret)
```

```python
import functools

import numpy as np

import jax
import jax.numpy as jnp
from jax import lax
from jax.experimental import pallas as pl
from jax.experimental.pallas import tpu as pltpu

F32 = jnp.float32
BF16 = jnp.bfloat16

D_MODEL = 1024
DEPTH = 2
GRID_W = 64
D_A = 512
LRU_BLOCKS = 8
LRU_BW = D_A // LRU_BLOCKS
CONV_W = 4
CONV_LEFT = 1
LRU_C = 8.0
D_B = 512
HG_HEADS = 4
HG_DK = D_B // HG_HEADS
HG_CHUNK = 32
F_MIN = 1e-20
D_C = 512
RET_HEADS = 4
RET_DK = D_C // RET_HEADS
ROPE_BASE = 10000.0
D_MIX = D_A + D_B + D_C
D_IN = 2 * D_A + 5 * D_B + 4 * D_C
EPS = 1e-6

COL_A = 0
COL_B = 2 * D_A
COL_C = COL_B + 5 * D_B

TM = 256
HEAD_W = 128
N_HEADS = 4
GATE_HALF = 256
RET_COL0 = 0
HG_COL0 = D_C
SAFE_LOG_DECAY = 70.0
V7X_VMEM_LIMIT_BYTES = 60000 * 1024
COND_ROWS = 8


def _sigmoid(x):
    return 0.5 * jnp.tanh(0.5 * x) + 0.5


def _silu(x):
    h = 0.5 * x
    return h * jnp.tanh(h) + h


def _dot(a, b):
    return jnp.dot(a, b, preferred_element_type=F32)


def _dot_nt(a, b):
    return lax.dot_general(a, b, (((1,), (1,)), ((), ())), preferred_element_type=F32)


def _dot_tn(a, b):
    return lax.dot_general(a, b, (((0,), (0,)), ((), ())), preferred_element_type=F32)


def _loop(n, body, unroll=1):
    if n == 1:
        body(0)
    else:
        def wrapped(i, carry):
            body(i)
            return carry
        lax.fori_loop(0, n, wrapped, 0, unroll=unroll)


def _mod_kernel(c_ref, w_ref, b_ref, o_ref):
    c = c_ref[...]
    o_ref[...] = _dot(_silu(c), w_ref[...]) + b_ref[...]


def _modulation(cond, w_mod, b_mod):
    n_col = 3 * D_MODEL
    blk = D_MODEL
    return pl.pallas_call(
        _mod_kernel,
        out_shape=jax.ShapeDtypeStruct((DEPTH, COND_ROWS, n_col), F32),
        grid=(DEPTH, n_col // blk),
        in_specs=[
            pl.BlockSpec((COND_ROWS, D_MODEL), lambda l, j: (0, 0)),
            pl.BlockSpec((None, D_MODEL, blk), lambda l, j: (l, 0, j)),
            pl.BlockSpec((None, 1, blk), lambda l, j: (l, 0, j)),
        ],
        out_specs=pl.BlockSpec((None, COND_ROWS, blk), lambda l, j: (l, 0, j)),
        name="modulation",
    )(cond, w_mod, b_mod.reshape(DEPTH, 1, n_col))


def _trunk_kernel(*refs, seq, layers, latent, merged):
    n_tiles = seq // TM
    n_chunks = seq // HG_CHUNK
    n_local = len(layers)
    ret_carried = latent or n_tiles > 1
    it = iter(refs)
    x_ref = next(it); mod_ref = next(it); ng_ref = next(it); w_in_ref = next(it)
    cw_ref = next(it); cb_ref = next(it); wblk_ref = next(it); gb_ref = next(it); lam_ref = next(it)
    lbl_ref = next(it); hgn_ref = next(it); rdl_ref = next(it); rtn_ref = next(it)
    w_out_ref = next(it); fg_ref = next(it)
    if latent:
        s_lru_ref = next(it); s_hg_ref = next(it); s_ret_ref = next(it)
        cos_ref = next(it); sina_ref = next(it); sinb_ref = next(it)
    y_ref = next(it)
    if not latent:
        f_lru_ref = next(it); f_hg_ref = next(it); f_ret_ref = next(it)
    xn_s = next(it); ycat_s = next(it); xpad_s = next(it)
    fa_s = next(it); fb_s = next(it); fc_s = next(it); fd_s = next(it)
    hv_s = next(it); hq_s = next(it); hqb_s = next(it); hkd_s = next(it)
    ebl_s = next(it); st_s = next(it)
    ds_s = next(it); wtab_s = next(it); gc_s = next(it); kv_s = next(it)
    xmid_s = next(it) if n_local > 1 else None
    flag_s = next(it)
    if merged:
        hg_if_s = next(it); hg_ib_s = next(it)
        tmp_b_s, tmp_q_s, tmp_k_s = hg_if_s, hg_ib_s, xpad_s
        lru_hf_s, lru_hb_s = fa_s, fc_s
    else:
        lru_hf_s, lru_hb_s = xpad_s, fa_s
        tmp_b_s, tmp_q_s, tmp_k_s = fa_s, fc_s, xpad_s
        hg_if_s, hg_ib_s = fb_s, fd_s

    def rows_of(i):
        return pl.ds(pl.multiple_of(i * TM, TM), TM)

    def head_cols(col0, h):
        return slice(col0 + h * HEAD_W, col0 + (h + 1) * HEAD_W)

    @pl.when(pl.program_id(0) == 0)
    def _():
        ti = lax.broadcasted_iota(jnp.int32, (TM, TM), 0)
        si = lax.broadcasted_iota(jnp.int32, (TM, TM), 1)
        rel = (ti - si).astype(F32)
        tcol = lax.broadcasted_iota(jnp.int32, (TM, HEAD_W), 0).astype(F32)
        for li in range(n_local):
            lg_all = -jax.nn.softplus(-rdl_ref[li])
            for h in range(N_HEADS):
                lg_f = lg_all[h:h + 1, :]
                lg_b = lg_all[N_HEADS + h:N_HEADS + h + 1, :]
                d_f = jnp.where(ti >= si, jnp.exp(jnp.where(ti >= si, rel, 0.0) * lg_f), 0.0)
                d_b = jnp.where(si >= ti, jnp.exp(jnp.where(si >= ti, -rel, 0.0) * lg_b), 0.0)
                ds_s[li, h] = d_f + d_b
                lgf = lg_f[:, :HEAD_W]
                lgb = lg_b[:, :HEAD_W]
                wtab_s[li, 0, h] = jnp.exp((TM - 1.0 - tcol) * lgf)
                wtab_s[li, 1, h] = jnp.exp(tcol * lgb)
                if ret_carried:
                    wtab_s[li, 2, h] = jnp.exp((tcol + 1.0) * lgf)
                    wtab_s[li, 3, h] = jnp.exp((TM - tcol) * lgb)
                gc_s[li, 0, h] = jnp.broadcast_to(jnp.exp(TM * lgf), (8, HEAD_W))
                gc_s[li, 1, h] = jnp.broadcast_to(jnp.exp(TM * lgb), (8, HEAD_W))

    ri = lax.broadcasted_iota(jnp.int32, (TM, TM), 0)
    ci = lax.broadcasted_iota(jnp.int32, (TM, TM), 1)
    same_chunk = (ri // HG_CHUNK) == (ci // HG_CHUNK)
    keeps = (same_chunk & (ci <= ri), same_chunk & (ci >= ri))
    cum_mats = tuple(jnp.where(kp, 1.0, 0.0).astype(BF16) for kp in keeps)
    cpt = TM // HG_CHUNK
    rrow = lax.broadcasted_iota(jnp.int32, (HG_CHUNK, HEAD_W), 0)
    chunk_shape = (cpt, HG_CHUNK, D_B)
    last_row = (HG_CHUNK - 1, 0)
    mid_row = (HG_CHUNK // 2 - 1, HG_CHUNK // 2)
    lbl = lbl_ref[...]

    def run_layer(li, layer, src_ref, dst_ref):
        last = layer == DEPTH - 1
        shift = mod_ref[li, 0:1, :]
        scale = mod_ref[li, 1:2, :]
        gate = mod_ref[li, 2:3, :]

        ng = ng_ref[li]

        def norm_tile(i):
            rows = rows_of(i)
            x = src_ref[rows, :]
            ms = jnp.mean(x * x, axis=-1, keepdims=True)
            xn = (x * lax.rsqrt(ms + EPS)) * ng
            xn = xn * (1.0 + scale) + shift
            xn_s[rows, :] = xn.astype(BF16)

        def lru_proj_tile(i):
            rows = rows_of(i)
            pa = _dot(xn_s[rows, :], w_in_ref[li, :, COL_A:COL_A + 2 * D_A])
            xpad_s[pl.ds(pl.multiple_of(i * TM + 8, 8), TM), :] = pa[:, :D_A]
            ycat_s[rows, 0:D_A] = _silu(pa[:, D_A:]).astype(BF16)

        cw = cw_ref[li]
        cb = cb_ref[li]
        nsp = jax.nn.softplus(-lam_ref[li])
        a_refs = (fa_s, fc_s)
        b_refs = (fb_s, fd_s)

        def lru_gate_tile(i):
            rows = rows_of(i)
            win = xpad_s[pl.ds(pl.multiple_of(i * TM, TM), TM + 16), :]
            wn = TM + 16
            u = cb
            for k in range(CONV_W):
                off = k - CONV_LEFT
                sh = win if off == 0 else pltpu.roll(win, (-off) % wn, 0)
                u = u + sh[8:8 + TM, :] * cw[k:k + 1, :]
            for hf in range(D_A // GATE_HALF):
                cols = slice(hf * GATE_HALF, (hf + 1) * GATE_HALF)
                uh = u[:, cols]
                g4 = _dot(uh.astype(BF16), wblk_ref[li, hf])
                for d in range(2):
                    c0 = 2 * d * GATE_HALF
                    r = _sigmoid(g4[:, c0:c0 + GATE_HALF] + gb_ref[li, 2 * d:2 * d + 1, cols])
                    ig = _sigmoid(g4[:, c0 + GATE_HALF:c0 + 2 * GATE_HALF] + gb_ref[li, 2 * d + 1:2 * d + 2, cols])
                    h = jnp.tanh((-0.5 * LRU_C) * r * nsp[d:d + 1, cols])
                    inv = 1.0 / (1.0 - h)
                    nh = -h
                    root = jnp.where(nh > 0.0, nh * lax.rsqrt(nh), 0.0)
                    a_refs[d][rows, cols] = (1.0 + h) * inv
                    b_refs[d][rows, cols] = (2.0 * root * inv) * (ig * uh)

        if latent:
            h0f = s_lru_ref[li, 0:1, :]
            h0b = s_lru_ref[li, 1:2, :]
        else:
            h0f = jnp.zeros((1, D_A), F32)
            h0b = h0f

        def lru_fwd_steps(g, hf):
            tf = pl.multiple_of(g * 8, 8)
            for k in range(8):
                rf = pl.ds(tf + k, 1)
                hf = fa_s[rf, :] * hf + fb_s[rf, :]
                lru_hf_s[rf, :] = hf
            return hf

        def lru_bwd_steps(g, hb):
            tb = pl.multiple_of(seq - 8 - g * 8, 8)
            for k in range(7, -1, -1):
                rb = pl.ds(tb + k, 1)
                hb = fc_s[rb, :] * hb + fd_s[rb, :]
                lru_hb_s[rb, :] = hb
            return hb

        def lru_steps(g, carry):
            return lru_fwd_steps(g, carry[0]), lru_bwd_steps(g, carry[1])

        def lru_finish(carry):
            if not latent:
                f_lru_ref[layer, 0:1, :] = carry[0]
                f_lru_ref[layer, 1:2, :] = carry[1]

        def lru_out_tile(i):
            rows = rows_of(i)
            ya = (lru_hf_s[rows, :] + lru_hb_s[rows, :]) * ycat_s[rows, 0:D_A].astype(F32)
            ycat_s[rows, 0:D_A] = ya.astype(BF16)

        lb_rows = []
        for d in range(2):
            ls = [lbl[k * 2 + d:k * 2 + d + 1, :] for k in range(DEPTH)]
            mx = functools.reduce(jnp.maximum, ls)
            es = [jnp.exp(v - mx) for v in ls]
            den = functools.reduce(lambda p, q: p + q, es)
            acc = jnp.zeros_like(mx)
            for k in range(1, layer + 1):
                acc = acc + es[k] / den
            lb_rows.append(acc)

        def hg_proj(i):
            return _dot(xn_s[rows_of(i), :], w_in_ref[li, :, COL_B:COL_B + 5 * D_B])

        def hg_gates(p, d):
            z = p[:, (1 + d) * D_B:(2 + d) * D_B]
            lb = lb_rows[d]
            sig = _sigmoid(z)
            f = lb + (1.0 - lb) * sig
            logf = jnp.log(jnp.maximum(f, F_MIN))
            k = (1.0 - lb) * (1.0 - sig)
            hi = logf.astype(BF16)
            lo = (logf - hi.astype(F32)).astype(BF16)
            b = _dot(cum_mats[d], hi) + _dot(cum_mats[d], lo)
            return k.reshape(chunk_shape), b.reshape(chunk_shape)

        def hg_tile(i):
            rows = rows_of(i)
            p = hg_proj(i)
            q3 = _silu(p[:, 0:D_B]).reshape(chunk_shape)
            v = p[:, 3 * D_B:4 * D_B].astype(BF16)
            hv_s[rows, :] = v
            ycat_s[rows, D_A:D_A + D_B] = _silu(p[:, 4 * D_B:5 * D_B]).astype(BF16)
            qmb, kib = [], []
            worst = None
            for d in range(2):
                k3, b3 = hg_gates(p, d)
                bl3 = b3[:, last_row[d]:last_row[d] + 1, :]
                m3 = b3[:, mid_row[d]:mid_row[d] + 1, :]
                rel = b3 - m3
                qm = q3 * jnp.exp(rel)
                ki = k3 * jnp.exp(-rel)
                hqb_s[d, rows, :] = (qm * jnp.exp(m3)).reshape(TM, D_B).astype(BF16)
                hkd_s[d, rows, :] = (ki * jnp.exp(bl3 - m3)).reshape(TM, D_B).astype(BF16)
                ebl_s[pl.ds(pl.multiple_of(d * n_chunks + i * cpt, cpt), cpt), :] = jnp.exp(bl3.reshape(cpt, D_B))
                span = jnp.maximum(jnp.max(-m3), jnp.max(m3 - bl3))
                worst = span if worst is None else jnp.maximum(worst, span)
                qmb.append(qm.reshape(TM, D_B).astype(BF16))
                kib.append(ki.reshape(TM, D_B).astype(BF16))
            for h in range(N_HEADS):
                hs = slice(h * HEAD_W, (h + 1) * HEAD_W)
                att = (jnp.where(keeps[0], _dot_nt(qmb[0][:, hs], kib[0][:, hs]), 0.0)
                       + jnp.where(keeps[1], _dot_nt(qmb[1][:, hs], kib[1][:, hs]), 0.0))
                dst_ref[rows, head_cols(HG_COL0, h)] = _dot(att.astype(BF16), v[:, hs])
            return worst > SAFE_LOG_DECAY

        def hg_tile_direct(i):
            rows = rows_of(i)
            p = hg_proj(i)
            q3 = _silu(p[:, 0:D_B]).reshape(chunk_shape)
            tmp_q_s[rows, :] = q3.reshape(TM, D_B)
            for d in range(2):
                k3, b3 = hg_gates(p, d)
                bl3 = b3[:, last_row[d]:last_row[d] + 1, :]
                hqb_s[d, rows, :] = (q3 * jnp.exp(b3)).reshape(TM, D_B).astype(BF16)
                hkd_s[d, rows, :] = (k3 * jnp.exp(bl3 - b3)).reshape(TM, D_B).astype(BF16)
                tmp_b_s[rows, :] = b3.reshape(TM, D_B)
                tmp_k_s[rows, :] = k3.reshape(TM, D_B)

                def chunk(c):
                    rws = pl.ds(pl.multiple_of(i * TM + c * HG_CHUNK, HG_CHUNK), HG_CHUNK)
                    for h in range(N_HEADS):
                        hs = slice(h * HEAD_W, (h + 1) * HEAD_W)
                        bf = tmp_b_s[rws, hs]
                        qf = tmp_q_s[rws, hs]
                        kf = tmp_k_s[rws, hs]
                        vf = hv_s[rws, hs].astype(F32)

                        def src(s, acc):
                            pick = rrow == s
                            ks = jnp.sum(jnp.where(pick, kf, 0.0), axis=0, keepdims=True)
                            vs = jnp.sum(jnp.where(pick, vf, 0.0), axis=0, keepdims=True)
                            bs = jnp.sum(jnp.where(pick, bf, 0.0), axis=0, keepdims=True)
                            w = jnp.sum(qf * ks * jnp.exp(jnp.minimum(bf - bs, 0.0)), axis=-1, keepdims=True)
                            keep = (rrow[:, 0:1] >= s) if d == 0 else (rrow[:, 0:1] <= s)
                            return acc + jnp.where(keep, w, 0.0) * vs

                        o = lax.fori_loop(0, HG_CHUNK, src, jnp.zeros((HG_CHUNK, HEAD_W), F32))
                        cs = head_cols(HG_COL0, h)
                        dst_ref[rws, cs] = o if d == 0 else dst_ref[rws, cs] + o

                _loop(cpt, chunk)

        def hg_state_init():
            for d in range(2):
                for h in range(N_HEADS):
                    if latent:
                        st_s[d * N_HEADS + h] = s_hg_ref[li, d, h].T
                    else:
                        st_s[d * N_HEADS + h] = jnp.zeros((HEAD_W, HEAD_W), F32)

        inter_refs = (hg_if_s, hg_ib_s)

        def hg_scan(j):
            for d in range(2):
                c = j if d == 0 else n_chunks - 1 - j
                rws = pl.ds(pl.multiple_of(c * HG_CHUNK, HG_CHUNK), HG_CHUNK)
                ebl = ebl_s[pl.ds(d * n_chunks + c, 1), :]
                for h in range(N_HEADS):
                    hs = slice(h * HEAD_W, (h + 1) * HEAD_W)
                    st = st_s[d * N_HEADS + h]
                    inter_refs[d][rws, hs] = _dot_nt(hqb_s[d, rws, hs], st.astype(BF16))
                    st_s[d * N_HEADS + h] = st * ebl[:, hs] + _dot_tn(hv_s[rws, hs], hkd_s[d, rws, hs])

        def hg_finish():
            if not latent:
                for d in range(2):
                    for h in range(N_HEADS):
                        f_hg_ref[layer, d, h] = st_s[d * N_HEADS + h].T

        def head_norm_gate(y, gain, col0, rows):
            for h in range(N_HEADS):
                hs = slice(h * HEAD_W, (h + 1) * HEAD_W)
                yh = y[:, hs]
                ms = jnp.mean(yh * yh, axis=-1, keepdims=True)
                yn = yh * lax.rsqrt(ms + EPS) * gain[:, hs]
                cs = slice(col0 + h * HEAD_W, col0 + (h + 1) * HEAD_W)
                ycat_s[rows, cs] = (yn * ycat_s[rows, cs].astype(F32)).astype(BF16)

        hgn = hgn_ref[li]

        def hg_out_tile(i):
            rows = rows_of(i)
            y = dst_ref[rows, HG_COL0:HG_COL0 + D_B] + hg_if_s[rows, :] + hg_ib_s[rows, :]
            head_norm_gate(y, hgn, D_A, rows)

        def ret_tile(i):
            rows = rows_of(i)
            p = _dot(xn_s[rows, :], w_in_ref[li, :, COL_C:COL_C + 4 * D_C])
            ycat_s[rows, D_A + D_B:D_MIX] = _silu(p[:, 3 * D_C:4 * D_C]).astype(BF16)
            for h in range(N_HEADS):
                hs = slice(h * HEAD_W, (h + 1) * HEAD_W)
                qh = p[:, hs]
                kh = p[:, D_C + h * HEAD_W:D_C + (h + 1) * HEAD_W]
                vh = p[:, 2 * D_C + h * HEAD_W:2 * D_C + (h + 1) * HEAD_W].astype(BF16)
                if latent:
                    cos = cos_ref[rows, :]
                    sa = sina_ref[rows, :]
                    sb = sinb_ref[rows, :]
                    qh = qh * cos + pltpu.roll(qh, HEAD_W - 32, 1) * sa + pltpu.roll(qh, 32, 1) * sb
                    kh = kh * cos + pltpu.roll(kh, HEAD_W - 32, 1) * sa + pltpu.roll(kh, 32, 1) * sb
                qh = qh * (RET_DK ** -0.5)
                qb = qh.astype(BF16)
                hq_s[rows, hs] = qb
                sc = _dot_nt(qb, kh.astype(BF16)) * ds_s[li, h]
                dst_ref[rows, head_cols(RET_COL0, h)] = _dot(sc.astype(BF16), vh)
                for d in range(2):
                    kw = (kh * wtab_s[li, d, h]).astype(BF16)
                    kv_s[(i * 2 + d) * N_HEADS + h] = _dot_tn(kw, vh)

        def ret_carry():
            for d in range(2):
                for h in range(N_HEADS):
                    hs = slice(h * HEAD_W, (h + 1) * HEAD_W)
                    if latent:
                        state = s_ret_ref[li, d, h]
                    else:
                        state = jnp.zeros((HEAD_W, HEAD_W), F32)
                    g_c = gc_s[li, d, h][0:1, :]
                    order = range(n_tiles) if d == 0 else range(n_tiles - 1, -1, -1)
                    for n in order:
                        rows = pl.ds(n * TM, TM)
                        if ret_carried:
                            qw =(hq_s[rows, hs].astype(F32) * wtab_s[li, 2 + d, h]).astype(BF16)
                            cs = head_cols(RET_COL0, h)
                            dst_ref[rows, cs] = dst_ref[rows, cs] + _dot(qw, state.astype(BF16))
                        state = state * g_c + kv_s[(n * 2 + d) * N_HEADS + h]
                    if not latent:
                        f_ret_ref[layer, d, h] = state

        rtn = rtn_ref[li]

        def ret_out_tile(i):
            rows = rows_of(i)
            head_norm_gate(dst_ref[rows, RET_COL0:RET_COL0 + D_C], rtn, D_A + D_B, rows)

        fg = fg_ref[...]

        def out_tile(i):
            rows = rows_of(i)
            delta = _dot(ycat_s[rows, :], w_out_ref[li])
            out = src_ref[rows, :] + gate * delta
            if last:
                ms = jnp.mean(out * out, axis=-1, keepdims=True)
                out = out * lax.rsqrt(ms + EPS) * fg
            dst_ref[rows, :] = out

        zero_pad = jnp.zeros((8, D_A), F32)
        xpad_s[pl.ds(0, 8), :] = zero_pad
        xpad_s[pl.ds(seq + 8, 8), :] = zero_pad

        def redo_tile_if(unsafe, i):
            @pl.when(unsafe)
            def _():
                hg_tile_direct(i)

        def front_tile(i):
            norm_tile(i)
            lru_proj_tile(i)

        def mixer_tile(i):
            flag_s[i] = hg_tile(i).astype(jnp.int32)
            ret_tile(i)
            lru_gate_tile(i)

        def redo_tile(i):
            redo_tile_if(flag_s[i] == 1, i)

        def scan_chunk(j, carry):
            hg_scan(j)
            for g in range(HG_CHUNK // 8):
                carry = lru_steps(j * (HG_CHUNK // 8) + g, carry)
            return carry

        def back_tile(i):
            if merged:
                lru_out_tile(i)
            hg_out_tile(i)
            ret_out_tile(i)
            out_tile(i)

        _loop(n_tiles, front_tile)
        _loop(n_tiles, mixer_tile)
        if merged:
            _loop(n_tiles, redo_tile)
            hg_state_init()
            lru_finish(lax.fori_loop(0, n_chunks, scan_chunk, (h0f, h0b), unroll=n_chunks))
        else:
            hf_fin = lax.fori_loop(0, seq // 8, lru_fwd_steps, h0f)
            hb_fin = lax.fori_loop(0, seq // 8, lru_bwd_steps, h0b)
            lru_finish((hf_fin, hb_fin))
            _loop(n_tiles, lru_out_tile)
            _loop(n_tiles, redo_tile)
            hg_state_init()
            _loop(n_chunks, hg_scan, unroll=8)
        hg_finish()
        ret_carry()
        _loop(n_tiles, back_tile)

    for li, layer in enumerate(layers):
        src = x_ref if li == 0 else xmid_s
        dst = y_ref if li == n_local - 1 else xmid_s
        run_layer(li, layer, src, dst)


def _trunk_call(x, mods, layers, latent, prm, states, rope):
    bsz, seq, _ = x.shape
    n_tiles = seq // TM
    n_chunks = seq // HG_CHUNK
    n_local = len(layers)
    l0 = layers[0]
    state_tail = (2, N_HEADS, HEAD_W, HEAD_W)

    def layers_spec(arr):
        tail = arr.shape[1:]
        zeros = (0,) * len(tail)
        return pl.BlockSpec((n_local,) + tail, lambda b: (l0 // n_local,) + zeros, pipeline_mode=pl.Buffered(1))

    def whole_spec(arr):
        zeros = (0,) * arr.ndim
        return pl.BlockSpec(arr.shape, lambda b: zeros, pipeline_mode=pl.Buffered(1))

    def per_seq_layers(tail, n_lay, first):
        zeros = (0,) * len(tail)
        return pl.BlockSpec((None, n_lay) + tail, lambda b: (b, first // n_lay) + zeros)

    act_mode = dict(pipeline_mode=pl.Buffered(1)) if latent else {}
    act_spec = pl.BlockSpec((None, seq, D_MODEL), lambda b: (b, 0, 0), **act_mode)
    cond_row = (lambda b: (l0 // n_local, 1 + b, 0, 0)) if latent else (lambda b: (l0 // n_local, 0, 0, 0))
    mod_spec = pl.BlockSpec((n_local, None, 3, D_MODEL), cond_row)

    stacked = [prm[k] for k in ("norm_g", "w_in", "conv_w", "conv_b", "wblk", "gbias", "lam")]
    operands = [x, mods] + stacked + [prm["lb_logits"], prm["hgrn_norm_g"], prm["rdl"], prm["ret_norm_g"],
                                      prm["w_out"], prm["final_g"]]
    in_specs = ([act_spec, mod_spec] + [layers_spec(a) for a in stacked]
                + [whole_spec(prm["lb_logits"]), layers_spec(prm["hgrn_norm_g"]), layers_spec(prm["rdl"]),
                   layers_spec(prm["ret_norm_g"]), layers_spec(prm["w_out"]), whole_spec(prm["final_g"])])
    if latent:
        operands += list(states) + list(rope)
        in_specs += [per_seq_layers((2, D_A), n_local, l0), per_seq_layers(state_tail, n_local, l0),
                     per_seq_layers(state_tail, n_local, l0)]
        in_specs += [whole_spec(r) for r in rope]
    out_shape = [jax.ShapeDtypeStruct((bsz, seq, D_MODEL), F32)]
    out_specs = [act_spec]
    if not latent:
        assert n_local == DEPTH, "the state outputs are written whole, one call must cover every layer"
        out_shape += [jax.ShapeDtypeStruct((bsz, DEPTH, 2, D_A), F32),
                      jax.ShapeDtypeStruct((bsz, DEPTH) + state_tail, F32),
                      jax.ShapeDtypeStruct((bsz, DEPTH) + state_tail, F32)]
        out_specs += [per_seq_layers((2, D_A), DEPTH, 0), per_seq_layers(state_tail, DEPTH, 0),
                      per_seq_layers(state_tail, DEPTH, 0)]
    scratch = [
        pltpu.VMEM((seq, D_MODEL), BF16),
        pltpu.VMEM((seq, D_MIX), BF16),
        pltpu.VMEM((seq + 16, D_A), F32),
        pltpu.VMEM((seq, D_A), F32), pltpu.VMEM((seq, D_A), F32),
        pltpu.VMEM((seq, D_A), F32), pltpu.VMEM((seq, D_A), F32),
        pltpu.VMEM((seq, D_B), BF16), pltpu.VMEM((seq, D_B), BF16),
        pltpu.VMEM((2, seq, D_B), BF16), pltpu.VMEM((2, seq, D_B), BF16),
        pltpu.VMEM((2 * n_chunks, D_B), F32),
        pltpu.VMEM((2 * N_HEADS, HEAD_W, HEAD_W), F32),
        pltpu.VMEM((n_local, N_HEADS, TM, TM), F32),
        pltpu.VMEM((n_local, 4 if (latent or n_tiles > 1) else 2, N_HEADS, TM, HEAD_W), F32),
        pltpu.VMEM((n_local, 2, N_HEADS, 8, HEAD_W), F32),
        pltpu.VMEM((n_tiles * 2 * N_HEADS, HEAD_W, HEAD_W), F32),
    ]
    if n_local > 1:
        scratch.append(pltpu.VMEM((seq, D_MODEL), F32))
    scratch += [pltpu.SMEM((n_tiles,), jnp.int32)]
    merged = n_chunks <= 8
    if merged:
        scratch += [pltpu.VMEM((seq, D_B), F32)] * 2
    body = functools.partial(_trunk_kernel, seq=seq, layers=tuple(layers), latent=latent, merged=merged)
    return pl.pallas_call(
        body,
        out_shape=out_shape,
        grid=(bsz,),
        in_specs=in_specs,
        out_specs=out_specs,
        scratch_shapes=scratch,
        compiler_params=pltpu.CompilerParams(dimension_semantics=("arbitrary",),
                                             vmem_limit_bytes=V7X_VMEM_LIMIT_BYTES),
        name=("latent" if latent else "context") + "_layers" + "".join(str(l) for l in layers),
    )(*operands)


def _rope_tables(seq):
    half = RET_DK // 2
    nf = half // 2
    t = np.arange(seq)
    freq = (ROPE_BASE ** (-np.arange(nf, dtype=np.float32) / nf)).astype(np.float32)
    ang_r = (t // GRID_W).astype(np.float32)[:, None] * freq[None, :]
    ang_c = (t % GRID_W).astype(np.float32)[:, None] * freq[None, :]
    zero = np.zeros((seq, nf), np.float32)
    cos = np.concatenate([np.cos(ang_r)] * 2 + [np.cos(ang_c)] * 2, axis=-1)
    sin_a = np.concatenate([-np.sin(ang_r), zero, -np.sin(ang_c), zero], axis=-1)
    sin_b = np.concatenate([zero, np.sin(ang_r), zero, np.sin(ang_c)], axis=-1)
    return tuple(jnp.asarray(a, F32) for a in (cos, sin_a, sin_b))


def _block_diag_gates(wa, wx):
    per_half = GATE_HALF // LRU_BW
    n_half = D_A // GATE_HALF
    eye = jnp.eye(per_half, dtype=F32)
    cols = []
    for d in range(2):
        for w in (wa, wx):
            blk = w[:, d].reshape(DEPTH, n_half, per_half, LRU_BW, LRU_BW)
            dense = jnp.einsum('pq,lhpij->lhpiqj', eye, blk).reshape(DEPTH, n_half, GATE_HALF, GATE_HALF)
            cols.append(dense)
    return jnp.concatenate(cols, axis=-1).astype(BF16)


def kernel(x_prompt, x_sample, state_lru, state_hgrn, state_ret, c, c_ctx, norm_g, w_mod, b_mod, w_in, conv_w,
           conv_b, lru_wa, lru_ba, lru_wx, lru_bx, lru_lambda, hgrn_lb_logits, hgrn_norm_g, ret_decay_logit,
           ret_norm_g, w_out, final_g):
    dec_b = x_sample.shape[0]
    pad = jnp.zeros((COND_ROWS - 1 - dec_b, D_MODEL), F32)
    cond = jnp.concatenate([c_ctx[None, :], c, pad], axis=0)
    mods = _modulation(cond, w_mod, b_mod).reshape(DEPTH, COND_ROWS, 3, D_MODEL)

    prm = dict(
        norm_g=norm_g[:, None, :],
        w_in=w_in.astype(BF16),
        conv_w=conv_w,
        conv_b=conv_b[:, None, :],
        wblk=_block_diag_gates(lru_wa, lru_wx),
        gbias=jnp.stack([lru_ba[:, 0], lru_bx[:, 0], lru_ba[:, 1], lru_bx[:, 1]], axis=1),
        lam=lru_lambda,
        lb_logits=hgrn_lb_logits.reshape(DEPTH * 2, D_B),
        hgrn_norm_g=hgrn_norm_g[:, None, :],
        rdl=jnp.broadcast_to(ret_decay_logit.reshape(DEPTH, 2 * RET_HEADS, 1), (DEPTH, 2 * RET_HEADS, TM)),
        ret_norm_g=ret_norm_g[:, None, :],
        w_out=w_out.astype(BF16),
        final_g=final_g[None, :],
    )

    y_prompt, new_lru, new_hgrn, new_ret = _trunk_call(x_prompt, mods, list(range(DEPTH)), False, prm, None, None)

    rope = _rope_tables(x_sample.shape[1])
    z = x_sample
    for l in range(DEPTH):
        (z,) = _trunk_call(z, mods, [l], True, prm, (state_lru, state_hgrn, state_ret), rope)
    y_sample = z

    return (y_prompt, y_sample, new_lru, new_hgrn, new_ret)
```

```python
import functools

import numpy as np

import jax
import jax.numpy as jnp
from jax import lax
from jax.experimental import pallas as pl
from jax.experimental.pallas import tpu as pltpu

F32 = jnp.float32
BF16 = jnp.bfloat16

D_MODEL = 1024
DEPTH = 2
GRID_W = 64
D_A = 512
LRU_BLOCKS = 8
LRU_BW = D_A // LRU_BLOCKS
CONV_W = 4
CONV_LEFT = 1
LRU_C = 8.0
D_B = 512
HG_HEADS = 4
HG_DK = D_B // HG_HEADS
HG_CHUNK = 32
F_MIN = 1e-20
D_C = 512
RET_HEADS = 4
RET_DK = D_C // RET_HEADS
ROPE_BASE = 10000.0
D_MIX = D_A + D_B + D_C
D_IN = 2 * D_A + 5 * D_B + 4 * D_C
EPS = 1e-6

COL_A = 0
COL_B = 2 * D_A
COL_C = COL_B + 5 * D_B

TM = 256
HEAD_W = 128
N_HEADS = 4
GATE_HALF = 256
RET_COL0 = 0
HG_COL0 = D_C
SAFE_LOG_DECAY = 70.0
V7X_VMEM_LIMIT_BYTES = 60000 * 1024
COND_ROWS = 8
MAX_STATIC_TRIPS = 4


def _sigmoid(x):
    return 0.5 * jnp.tanh(0.5 * x) + 0.5


def _silu(x):
    h = 0.5 * x
    return h * jnp.tanh(h) + h


def _dot(a, b):
    return jnp.dot(a, b, preferred_element_type=F32)


def _dot_nt(a, b):
    return lax.dot_general(a, b, (((1,), (1,)), ((), ())), preferred_element_type=F32)


def _dot_tn(a, b):
    return lax.dot_general(a, b, (((0,), (0,)), ((), ())), preferred_element_type=F32)


def _loop(n, body, unroll=1):
    if n <= MAX_STATIC_TRIPS:
        for i in range(n):
            body(i)
    else:
        def wrapped(i, carry):
            body(i)
            return carry
        lax.fori_loop(0, n, wrapped, 0, unroll=unroll)


def _mod_kernel(c_ref, w_ref, b_ref, o_ref):
    c = c_ref[...]
    o_ref[...] = _dot(_silu(c), w_ref[...]) + b_ref[...]


def _modulation(cond, w_mod, b_mod):
    n_col = 3 * D_MODEL
    blk = D_MODEL
    return pl.pallas_call(
        _mod_kernel,
        out_shape=jax.ShapeDtypeStruct((DEPTH, COND_ROWS, n_col), F32),
        grid=(DEPTH, n_col // blk),
        in_specs=[
            pl.BlockSpec((COND_ROWS, D_MODEL), lambda l, j: (0, 0)),
            pl.BlockSpec((None, D_MODEL, blk), lambda l, j: (l, 0, j)),
            pl.BlockSpec((None, 1, blk), lambda l, j: (l, 0, j)),
        ],
        out_specs=pl.BlockSpec((None, COND_ROWS, blk), lambda l, j: (l, 0, j)),
        name="modulation",
    )(cond, w_mod, b_mod.reshape(DEPTH, 1, n_col))


def _trunk_kernel(*refs, seq, layers, latent, merged):
    n_tiles = seq // TM
    n_chunks = seq // HG_CHUNK
    n_local = len(layers)
    ret_carried = latent or n_tiles > 1
    it = iter(refs)
    x_ref = next(it); mod_ref = next(it); ng_ref = next(it); w_in_ref = next(it)
    cw_ref = next(it); cb_ref = next(it); wblk_ref = next(it); gb_ref = next(it); lam_ref = next(it)
    lbl_ref = next(it); hgn_ref = next(it); rdl_ref = next(it); rtn_ref = next(it)
    w_out_ref = next(it); fg_ref = next(it)
    if latent:
        s_lru_ref = next(it); s_hg_ref = next(it); s_ret_ref = next(it)
        cos_ref = next(it); sina_ref = next(it); sinb_ref = next(it)
    y_ref = next(it)
    if not latent:
        f_lru_ref = next(it); f_hg_ref = next(it); f_ret_ref = next(it)
    xn_s = next(it); ycat_s = next(it); xpad_s = next(it)
    fa_s = next(it); fb_s = next(it); fc_s = next(it); fd_s = next(it)
    hv_s = next(it); hq_s = next(it); hqb_s = next(it); hkd_s = next(it)
    ebl_s = next(it); st_s = next(it)
    ds_s = next(it); wtab_s = next(it); gc_s = next(it); kv_s = next(it)
    xmid_s = next(it) if n_local > 1 else None
    flag_s = next(it)
    if merged:
        hg_if_s = next(it); hg_ib_s = next(it)
        tmp_b_s, tmp_q_s, tmp_k_s = hg_if_s, hg_ib_s, xpad_s
        lru_hf_s, lru_hb_s = fa_s, fc_s
    else:
        lru_hf_s, lru_hb_s = xpad_s, fa_s
        tmp_b_s, tmp_q_s, tmp_k_s = fa_s, fc_s, xpad_s
        hg_if_s, hg_ib_s = fb_s, fd_s

    def rows_of(i):
        return pl.ds(pl.multiple_of(i * TM, TM), TM)

    def head_cols(col0, h):
        return slice(col0 + h * HEAD_W, col0 + (h + 1) * HEAD_W)

    @pl.when(pl.program_id(0) == 0)
    def _():
        ti = lax.broadcasted_iota(jnp.int32, (TM, TM), 0)
        si = lax.broadcasted_iota(jnp.int32, (TM, TM), 1)
        rel = (ti - si).astype(F32)
        tcol = lax.broadcasted_iota(jnp.int32, (TM, HEAD_W), 0).astype(F32)
        for li in range(n_local):
            lg_all = -jax.nn.softplus(-rdl_ref[li])
            for h in range(N_HEADS):
                lg_f = lg_all[h:h + 1, :]
                lg_b = lg_all[N_HEADS + h:N_HEADS + h + 1, :]
                d_f = jnp.where(ti >= si, jnp.exp(jnp.where(ti >= si, rel, 0.0) * lg_f), 0.0)
                d_b = jnp.where(si >= ti, jnp.exp(jnp.where(si >= ti, -rel, 0.0) * lg_b), 0.0)
                ds_s[li, h] = d_f + d_b
                lgf = lg_f[:, :HEAD_W]
                lgb = lg_b[:, :HEAD_W]
                wtab_s[li, 0, h] = jnp.exp((TM - 1.0 - tcol) * lgf)
                wtab_s[li, 1, h] = jnp.exp(tcol * lgb)
                gc_s[li, 0, h] = jnp.broadcast_to(jnp.exp(TM * lgf), (8, HEAD_W))
                gc_s[li, 1, h] = jnp.broadcast_to(jnp.exp(TM * lgb), (8, HEAD_W))

    ri = lax.broadcasted_iota(jnp.int32, (TM, TM), 0)
    ci = lax.broadcasted_iota(jnp.int32, (TM, TM), 1)
    same_chunk = (ri // HG_CHUNK) == (ci // HG_CHUNK)
    keeps = (same_chunk & (ci <= ri), same_chunk & (ci >= ri))
    cum_mats = tuple(jnp.where(kp, 1.0, 0.0).astype(BF16) for kp in keeps)
    cpt = TM // HG_CHUNK
    rrow = lax.broadcasted_iota(jnp.int32, (HG_CHUNK, HEAD_W), 0)
    chunk_shape = (cpt, HG_CHUNK, D_B)
    last_row = (HG_CHUNK - 1, 0)
    mid_row = (HG_CHUNK // 2 - 1, HG_CHUNK // 2)
    lbl = lbl_ref[...]

    def run_layer(li, layer, src_ref, dst_ref):
        last = layer == DEPTH - 1
        shift = mod_ref[li, 0:1, :]
        scale = mod_ref[li, 1:2, :]
        gate = mod_ref[li, 2:3, :]

        ng = ng_ref[li]

        def norm_tile(i):
            rows = rows_of(i)
            x = src_ref[rows, :]
            ms = jnp.mean(x * x, axis=-1, keepdims=True)
            xn = (x * lax.rsqrt(ms + EPS)) * ng
            xn = xn * (1.0 + scale) + shift
            xn_s[rows, :] = xn.astype(BF16)

        def lru_proj_tile(i):
            rows = rows_of(i)
            pa = _dot(xn_s[rows, :], w_in_ref[li, :, COL_A:COL_A + 2 * D_A])
            xpad_s[pl.ds(pl.multiple_of(i * TM + 8, 8), TM), :] = pa[:, :D_A]
            ycat_s[rows, 0:D_A] = _silu(pa[:, D_A:]).astype(BF16)

        cw = cw_ref[li]
        cb = cb_ref[li]
        nsp = jax.nn.softplus(-lam_ref[li])
        a_refs = (fa_s, fc_s)
        b_refs = (fb_s, fd_s)

        def lru_gate_tile(i):
            rows = rows_of(i)
            win = xpad_s[pl.ds(pl.multiple_of(i * TM, TM), TM + 16), :]
            wn = TM + 16
            u = cb
            for k in range(CONV_W):
                off = k - CONV_LEFT
                sh = win if off == 0 else pltpu.roll(win, (-off) % wn, 0)
                u = u + sh[8:8 + TM, :] * cw[k:k + 1, :]
            for hf in range(D_A // GATE_HALF):
                cols = slice(hf * GATE_HALF, (hf + 1) * GATE_HALF)
                uh = u[:, cols]
                g4 = _dot(uh.astype(BF16), wblk_ref[li, hf])
                for d in range(2):
                    c0 = 2 * d * GATE_HALF
                    r = _sigmoid(g4[:, c0:c0 + GATE_HALF] + gb_ref[li, 2 * d:2 * d + 1, cols])
                    ig = _sigmoid(g4[:, c0 + GATE_HALF:c0 + 2 * GATE_HALF] + gb_ref[li, 2 * d + 1:2 * d + 2, cols])
                    h = jnp.tanh((-0.5 * LRU_C) * r * nsp[d:d + 1, cols])
                    inv = 1.0 / (1.0 - h)
                    nh = -h
                    root = jnp.where(nh > 0.0, nh * lax.rsqrt(nh), 0.0)
                    a_refs[d][rows, cols] = (1.0 + h) * inv
                    b_refs[d][rows, cols] = (2.0 * root * inv) * (ig * uh)

        if latent:
            h0f = s_lru_ref[li, 0:1, :]
            h0b = s_lru_ref[li, 1:2, :]
        else:
            h0f = jnp.zeros((1, D_A), F32)
            h0b = h0f

        def lru_fwd_steps(g, hf):
            tf = pl.multiple_of(g * 8, 8)
            for k in range(8):
                rf = pl.ds(tf + k, 1)
                hf = fa_s[rf, :] * hf + fb_s[rf, :]
                lru_hf_s[rf, :] = hf
            return hf

        def lru_bwd_steps(g, hb):
            tb = pl.multiple_of(seq - 8 - g * 8, 8)
            for k in range(7, -1, -1):
                rb = pl.ds(tb + k, 1)
                hb = fc_s[rb, :] * hb + fd_s[rb, :]
                lru_hb_s[rb, :] = hb
            return hb

        def lru_steps(g, carry):
            return lru_fwd_steps(g, carry[0]), lru_bwd_steps(g, carry[1])

        def lru_finish(carry):
            if not latent:
                f_lru_ref[layer, 0:1, :] = carry[0]
                f_lru_ref[layer, 1:2, :] = carry[1]

        def lru_out_tile(i):
            rows = rows_of(i)
            ya = (lru_hf_s[rows, :] + lru_hb_s[rows, :]) * ycat_s[rows, 0:D_A].astype(F32)
            ycat_s[rows, 0:D_A] = ya.astype(BF16)

        lb_rows = []
        for d in range(2):
            ls = [lbl[k * 2 + d:k * 2 + d + 1, :] for k in range(DEPTH)]
            mx = functools.reduce(jnp.maximum, ls)
            es = [jnp.exp(v - mx) for v in ls]
            den = functools.reduce(lambda p, q: p + q, es)
            acc = jnp.zeros_like(mx)
            for k in range(1, layer + 1):
                acc = acc + es[k] / den
            lb_rows.append(acc)

        def hg_proj(i):
            return _dot(xn_s[rows_of(i), :], w_in_ref[li, :, COL_B:COL_B + 5 * D_B])

        def hg_gates(p, d):
            z = p[:, (1 + d) * D_B:(2 + d) * D_B]
            lb = lb_rows[d]
            sig = _sigmoid(z)
            f = lb + (1.0 - lb) * sig
            logf = jnp.log(jnp.maximum(f, F_MIN))
            k = (1.0 - lb) * (1.0 - sig)
            hi = logf.astype(BF16)
            lo = (logf - hi.astype(F32)).astype(BF16)
            b = _dot(cum_mats[d], hi) + _dot(cum_mats[d], lo)
            return k.reshape(chunk_shape), b.reshape(chunk_shape)

        def hg_tile(i):
            rows = rows_of(i)
            p = hg_proj(i)
            q3 = _silu(p[:, 0:D_B]).reshape(chunk_shape)
            v = p[:, 3 * D_B:4 * D_B].astype(BF16)
            hv_s[rows, :] = v
            ycat_s[rows, D_A:D_A + D_B] = _silu(p[:, 4 * D_B:5 * D_B]).astype(BF16)
            qmb, kib = [], []
            worst = None
            for d in range(2):
                k3, b3 = hg_gates(p, d)
                bl3 = b3[:, last_row[d]:last_row[d] + 1, :]
                m3 = b3[:, mid_row[d]:mid_row[d] + 1, :]
                rel = b3 - m3
                qm = q3 * jnp.exp(rel)
                ki = k3 * jnp.exp(-rel)
                hqb_s[d, rows, :] = (qm * jnp.exp(m3)).reshape(TM, D_B).astype(BF16)
                hkd_s[d, rows, :] = (ki * jnp.exp(bl3 - m3)).reshape(TM, D_B).astype(BF16)
                ebl_s[pl.ds(pl.multiple_of(d * n_chunks + i * cpt, cpt), cpt), :] = jnp.exp(bl3.reshape(cpt, D_B))
                span = jnp.maximum(jnp.max(-m3), jnp.max(m3 - bl3))
                worst = span if worst is None else jnp.maximum(worst, span)
                qmb.append(qm.reshape(TM, D_B).astype(BF16))
                kib.append(ki.reshape(TM, D_B).astype(BF16))
            for h in range(N_HEADS):
                hs = slice(h * HEAD_W, (h + 1) * HEAD_W)
                att = (jnp.where(keeps[0], _dot_nt(qmb[0][:, hs], kib[0][:, hs]), 0.0)
                       + jnp.where(keeps[1], _dot_nt(qmb[1][:, hs], kib[1][:, hs]), 0.0))
                dst_ref[rows, head_cols(HG_COL0, h)] = _dot(att.astype(BF16), v[:, hs])
            return worst > SAFE_LOG_DECAY

        def hg_tile_direct(i):
            rows = rows_of(i)
            p = hg_proj(i)
            q3 = _silu(p[:, 0:D_B]).reshape(chunk_shape)
            tmp_q_s[rows, :] = q3.reshape(TM, D_B)
            for d in range(2):
                k3, b3 = hg_gates(p, d)
                bl3 = b3[:, last_row[d]:last_row[d] + 1, :]
                hqb_s[d, rows, :] = (q3 * jnp.exp(b3)).reshape(TM, D_B).astype(BF16)
                hkd_s[d, rows, :] = (k3 * jnp.exp(bl3 - b3)).reshape(TM, D_B).astype(BF16)
                tmp_b_s[rows, :] = b3.reshape(TM, D_B)
                tmp_k_s[rows, :] = k3.reshape(TM, D_B)

                def chunk(c):
                    rws = pl.ds(pl.multiple_of(i * TM + c * HG_CHUNK, HG_CHUNK), HG_CHUNK)
                    for h in range(N_HEADS):
                        hs = slice(h * HEAD_W, (h + 1) * HEAD_W)
                        bf = tmp_b_s[rws, hs]
                        qf = tmp_q_s[rws, hs]
                        kf = tmp_k_s[rws, hs]
                        vf = hv_s[rws, hs].astype(F32)

                        def src(s, acc):
                            pick = rrow == s
                            ks = jnp.sum(jnp.where(pick, kf, 0.0), axis=0, keepdims=True)
                            vs = jnp.sum(jnp.where(pick, vf, 0.0), axis=0, keepdims=True)
                            bs = jnp.sum(jnp.where(pick, bf, 0.0), axis=0, keepdims=True)
                            w = jnp.sum(qf * ks * jnp.exp(jnp.minimum(bf - bs, 0.0)), axis=-1, keepdims=True)
                            keep = (rrow[:, 0:1] >= s) if d == 0 else (rrow[:, 0:1] <= s)
                            return acc + jnp.where(keep, w, 0.0) * vs

                        o = lax.fori_loop(0, HG_CHUNK, src, jnp.zeros((HG_CHUNK, HEAD_W), F32))
                        cs = head_cols(HG_COL0, h)
                        dst_ref[rws, cs] = o if d == 0 else dst_ref[rws, cs] + o

                _loop(cpt, chunk)

        def hg_state_init():
            for d in range(2):
                for h in range(N_HEADS):
                    if latent:
                        st_s[d * N_HEADS + h] = s_hg_ref[li, d, h].T
                    else:
                        st_s[d * N_HEADS + h] = jnp.zeros((HEAD_W, HEAD_W), F32)

        inter_refs = (hg_if_s, hg_ib_s)

        def hg_scan(j):
            for d in range(2):
                c = j if d == 0 else n_chunks - 1 - j
                rws = pl.ds(pl.multiple_of(c * HG_CHUNK, HG_CHUNK), HG_CHUNK)
                ebl = ebl_s[pl.ds(d * n_chunks + c, 1), :]
                for h in range(N_HEADS):
                    hs = slice(h * HEAD_W, (h + 1) * HEAD_W)
                    st = st_s[d * N_HEADS + h]
                    inter_refs[d][rws, hs] = _dot_nt(hqb_s[d, rws, hs], st.astype(BF16))
                    st_s[d * N_HEADS + h] = st * ebl[:, hs] + _dot_tn(hv_s[rws, hs], hkd_s[d, rws, hs])

        def hg_finish():
            if not latent:
                for d in range(2):
                    for h in range(N_HEADS):
                        f_hg_ref[layer, d, h] = st_s[d * N_HEADS + h].T

        def head_norm_gate(y, gain, col0, rows):
            for h in range(N_HEADS):
                hs = slice(h * HEAD_W, (h + 1) * HEAD_W)
                yh = y[:, hs]
                ms = jnp.mean(yh * yh, axis=-1, keepdims=True)
                yn = yh * lax.rsqrt(ms + EPS) * gain[:, hs]
                cs = slice(col0 + h * HEAD_W, col0 + (h + 1) * HEAD_W)
                ycat_s[rows, cs] = (yn * ycat_s[rows, cs].astype(F32)).astype(BF16)

        hgn = hgn_ref[li]

        def hg_out_tile(i):
            rows = rows_of(i)
            y = dst_ref[rows, HG_COL0:HG_COL0 + D_B] + hg_if_s[rows, :] + hg_ib_s[rows, :]
            head_norm_gate(y, hgn, D_A, rows)

        def ret_tile(i):
            rows = rows_of(i)
            p = _dot(xn_s[rows, :], w_in_ref[li, :, COL_C:COL_C + 4 * D_C])
            ycat_s[rows, D_A + D_B:D_MIX] = _silu(p[:, 3 * D_C:4 * D_C]).astype(BF16)
            for h in range(N_HEADS):
                hs = slice(h * HEAD_W, (h + 1) * HEAD_W)
                qh = p[:, hs]
                kh = p[:, D_C + h * HEAD_W:D_C + (h + 1) * HEAD_W]
                vh = p[:, 2 * D_C + h * HEAD_W:2 * D_C + (h + 1) * HEAD_W].astype(BF16)
                if latent:
                    cos = cos_ref[rows, :]
                    sa = sina_ref[rows, :]
                    sb = sinb_ref[rows, :]
                    qh = qh * cos + pltpu.roll(qh, HEAD_W - 32, 1) * sa + pltpu.roll(qh, 32, 1) * sb
                    kh = kh * cos + pltpu.roll(kh, HEAD_W - 32, 1) * sa + pltpu.roll(kh, 32, 1) * sb
                qh = qh * (RET_DK ** -0.5)
                qb = qh.astype(BF16)
                hq_s[rows, hs] = qb
                sc = _dot_nt(qb, kh.astype(BF16)) * ds_s[li, h]
                dst_ref[rows, head_cols(RET_COL0, h)] = _dot(sc.astype(BF16), vh)
                for d in range(2):
                    kw = (kh * wtab_s[li, d, h]).astype(BF16)
                    kv_s[(i * 2 + d) * N_HEADS + h] = _dot_tn(kw, vh)

        def ret_carry():
            if ret_carried:
                lg_all = -jax.nn.softplus(-rdl_ref[li])[:, :HEAD_W]
                tcol = lax.broadcasted_iota(jnp.int32, (TM, HEAD_W), 0).astype(F32)
            for d in range(2):
                for h in range(N_HEADS):
                    hs = slice(h * HEAD_W, (h + 1) * HEAD_W)
                    if latent:
                        state = s_ret_ref[li, d, h]
                    else:
                        state = jnp.zeros((HEAD_W, HEAD_W), F32)
                    g_c = gc_s[li, d, h][0:1, :]
                    if ret_carried:
                        steps = tcol + 1.0 if d == 0 else TM - tcol
                        q_decay = jnp.exp(steps * lg_all[d * N_HEADS + h:d * N_HEADS + h + 1, :])
                    order = range(n_tiles) if d == 0 else range(n_tiles - 1, -1, -1)
                    for n in order:
                        rows = pl.ds(n * TM, TM)
                        if ret_carried:
                            qw = (hq_s[rows, hs].astype(F32) * q_decay).astype(BF16)
                            cs = head_cols(RET_COL0, h)
                            dst_ref[rows, cs] = dst_ref[rows, cs] + _dot(qw, state.astype(BF16))
                        state = state * g_c + kv_s[(n * 2 + d) * N_HEADS + h]
                    if not latent:
                        f_ret_ref[layer, d, h] = state

        rtn = rtn_ref[li]

        def ret_out_tile(i):
            rows = rows_of(i)
            head_norm_gate(dst_ref[rows, RET_COL0:RET_COL0 + D_C], rtn, D_A + D_B, rows)

        fg = fg_ref[...]

        def out_tile(i):
            rows = rows_of(i)
            delta = _dot(ycat_s[rows, :], w_out_ref[li])
            out = src_ref[rows, :] + gate * delta
            if last:
                ms = jnp.mean(out * out, axis=-1, keepdims=True)
                out = out * lax.rsqrt(ms + EPS) * fg
            dst_ref[rows, :] = out

        zero_pad = jnp.zeros((8, D_A), F32)
        xpad_s[pl.ds(0, 8), :] = zero_pad
        xpad_s[pl.ds(seq + 8, 8), :] = zero_pad

        def redo_tile_if(unsafe, i):
            @pl.when(unsafe)
            def _():
                hg_tile_direct(i)

        def front_tile(i):
            norm_tile(i)
            lru_proj_tile(i)

        def mixer_tile(i):
            flag_s[i] = hg_tile(i).astype(jnp.int32)
            ret_tile(i)
            lru_gate_tile(i)

        def redo_tile(i):
            redo_tile_if(flag_s[i] == 1, i)

        def scan_chunk(j, carry):
            hg_scan(j)
            for g in range(HG_CHUNK // 8):
                carry = lru_steps(j * (HG_CHUNK // 8) + g, carry)
            return carry

        def back_tile(i):
            if merged:
                lru_out_tile(i)
            hg_out_tile(i)
            ret_out_tile(i)
            out_tile(i)

        _loop(n_tiles, front_tile)
        _loop(n_tiles, mixer_tile)
        if merged:
            _loop(n_tiles, redo_tile)
            hg_state_init()
            lru_finish(lax.fori_loop(0, n_chunks, scan_chunk, (h0f, h0b), unroll=n_chunks))
        else:
            hf_fin = lax.fori_loop(0, seq // 8, lru_fwd_steps, h0f, unroll=4)
            hb_fin = lax.fori_loop(0, seq // 8, lru_bwd_steps, h0b, unroll=4)
            lru_finish((hf_fin, hb_fin))
            _loop(n_tiles, lru_out_tile)
            _loop(n_tiles, redo_tile)
            hg_state_init()
            _loop(n_chunks, hg_scan, unroll=16)
        hg_finish()
        ret_carry()
        _loop(n_tiles, back_tile)

    for li, layer in enumerate(layers):
        src = x_ref if li == 0 else xmid_s
        dst = y_ref if li == n_local - 1 else xmid_s
        run_layer(li, layer, src, dst)


def _trunk_call(x, mods, layers, latent, prm, states, rope):
    bsz, seq, _ = x.shape
    n_tiles = seq // TM
    n_chunks = seq // HG_CHUNK
    n_local = len(layers)
    l0 = layers[0]
    state_tail = (2, N_HEADS, HEAD_W, HEAD_W)

    def layers_spec(arr):
        tail = arr.shape[1:]
        zeros = (0,) * len(tail)
        return pl.BlockSpec((n_local,) + tail, lambda b: (l0 // n_local,) + zeros, pipeline_mode=pl.Buffered(1))

    def whole_spec(arr):
        zeros = (0,) * arr.ndim
        return pl.BlockSpec(arr.shape, lambda b: zeros, pipeline_mode=pl.Buffered(1))

    def per_seq_layers(tail, n_lay, first):
        zeros = (0,) * len(tail)
        return pl.BlockSpec((None, n_lay) + tail, lambda b: (b, first // n_lay) + zeros)

    act_mode = dict(pipeline_mode=pl.Buffered(1)) if latent else {}
    act_spec = pl.BlockSpec((None, seq, D_MODEL), lambda b: (b, 0, 0), **act_mode)
    cond_row = (lambda b: (l0 // n_local, 1 + b, 0, 0)) if latent else (lambda b: (l0 // n_local, 0, 0, 0))
    mod_spec = pl.BlockSpec((n_local, None, 3, D_MODEL), cond_row)

    stacked = [prm[k] for k in ("norm_g", "w_in", "conv_w", "conv_b", "wblk", "gbias", "lam")]
    operands = [x, mods] + stacked + [prm["lb_logits"], prm["hgrn_norm_g"], prm["rdl"], prm["ret_norm_g"],
                                      prm["w_out"], prm["final_g"]]
    in_specs = ([act_spec, mod_spec] + [layers_spec(a) for a in stacked]
                + [whole_spec(prm["lb_logits"]), layers_spec(prm["hgrn_norm_g"]), layers_spec(prm["rdl"]),
                   layers_spec(prm["ret_norm_g"]), layers_spec(prm["w_out"]), whole_spec(prm["final_g"])])
    if latent:
        operands += list(states) + list(rope)
        in_specs += [per_seq_layers((2, D_A), n_local, l0), per_seq_layers(state_tail, n_local, l0),
                     per_seq_layers(state_tail, n_local, l0)]
        in_specs += [whole_spec(r) for r in rope]
    out_shape = [jax.ShapeDtypeStruct((bsz, seq, D_MODEL), F32)]
    out_specs = [act_spec]
    if not latent:
        assert n_local == DEPTH, "the state outputs are written whole, one call must cover every layer"
        out_shape += [jax.ShapeDtypeStruct((bsz, DEPTH, 2, D_A), F32),
                      jax.ShapeDtypeStruct((bsz, DEPTH) + state_tail, F32),
                      jax.ShapeDtypeStruct((bsz, DEPTH) + state_tail, F32)]
        out_specs += [per_seq_layers((2, D_A), DEPTH, 0), per_seq_layers(state_tail, DEPTH, 0),
                      per_seq_layers(state_tail, DEPTH, 0)]
    scratch = [
        pltpu.VMEM((seq, D_MODEL), BF16),
        pltpu.VMEM((seq, D_MIX), BF16),
        pltpu.VMEM((seq + 16, D_A), F32),
        pltpu.VMEM((seq, D_A), F32), pltpu.VMEM((seq, D_A), F32),
        pltpu.VMEM((seq, D_A), F32), pltpu.VMEM((seq, D_A), F32),
        pltpu.VMEM((seq, D_B), BF16), pltpu.VMEM((seq, D_B), BF16),
        pltpu.VMEM((2, seq, D_B), BF16), pltpu.VMEM((2, seq, D_B), BF16),
        pltpu.VMEM((2 * n_chunks, D_B), F32),
        pltpu.VMEM((2 * N_HEADS, HEAD_W, HEAD_W), F32),
        pltpu.VMEM((n_local, N_HEADS, TM, TM), F32),
        pltpu.VMEM((n_local, 2, N_HEADS, TM, HEAD_W), F32),
        pltpu.VMEM((n_local, 2, N_HEADS, 8, HEAD_W), F32),
        pltpu.VMEM((n_tiles * 2 * N_HEADS, HEAD_W, HEAD_W), F32),
    ]
    if n_local > 1:
        scratch.append(pltpu.VMEM((seq, D_MODEL), F32))
    scratch += [pltpu.SMEM((n_tiles,), jnp.int32)]
    merged = n_chunks <= 8
    if merged:
        scratch += [pltpu.VMEM((seq, D_B), F32)] * 2
    body = functools.partial(_trunk_kernel, seq=seq, layers=tuple(layers), latent=latent, merged=merged)
    return pl.pallas_call(
        body,
        out_shape=out_shape,
        grid=(bsz,),
        in_specs=in_specs,
        out_specs=out_specs,
        scratch_shapes=scratch,
        compiler_params=pltpu.CompilerParams(dimension_semantics=("arbitrary",),
                                             vmem_limit_bytes=V7X_VMEM_LIMIT_BYTES),
        name=("latent" if latent else "context") + "_layers" + "".join(str(l) for l in layers),
    )(*operands)


def _rope_tables(seq):
    half = RET_DK // 2
    nf = half // 2
    t = np.arange(seq)
    freq = (ROPE_BASE ** (-np.arange(nf, dtype=np.float32) / nf)).astype(np.float32)
    ang_r = (t // GRID_W).astype(np.float32)[:, None] * freq[None, :]
    ang_c = (t % GRID_W).astype(np.float32)[:, None] * freq[None, :]
    zero = np.zeros((seq, nf), np.float32)
    cos = np.concatenate([np.cos(ang_r)] * 2 + [np.cos(ang_c)] * 2, axis=-1)
    sin_a = np.concatenate([-np.sin(ang_r), zero, -np.sin(ang_c), zero], axis=-1)
    sin_b = np.concatenate([zero, np.sin(ang_r), zero, np.sin(ang_c)], axis=-1)
    return tuple(jnp.asarray(a, F32) for a in (cos, sin_a, sin_b))


def _block_diag_gates(wa, wx):
    per_half = GATE_HALF // LRU_BW
    n_half = D_A // GATE_HALF
    eye = jnp.eye(per_half, dtype=F32)
    cols = []
    for d in range(2):
        for w in (wa, wx):
            blk = w[:, d].reshape(DEPTH, n_half, per_half, LRU_BW, LRU_BW)
            dense = jnp.einsum('pq,lhpij->lhpiqj', eye, blk).reshape(DEPTH, n_half, GATE_HALF, GATE_HALF)
            cols.append(dense)
    return jnp.concatenate(cols, axis=-1).astype(BF16)


def kernel(x_prompt, x_sample, state_lru, state_hgrn, state_ret, c, c_ctx, norm_g, w_mod, b_mod, w_in, conv_w,
           conv_b, lru_wa, lru_ba, lru_wx, lru_bx, lru_lambda, hgrn_lb_logits, hgrn_norm_g, ret_decay_logit,
           ret_norm_g, w_out, final_g):
    dec_b = x_sample.shape[0]
    pad = jnp.zeros((COND_ROWS - 1 - dec_b, D_MODEL), F32)
    cond = jnp.concatenate([c_ctx[None, :], c, pad], axis=0)
    mods = _modulation(cond, w_mod, b_mod).reshape(DEPTH, COND_ROWS, 3, D_MODEL)

    prm = dict(
        norm_g=norm_g[:, None, :],
        w_in=w_in.astype(BF16),
        conv_w=conv_w,
        conv_b=conv_b[:, None, :],
        wblk=_block_diag_gates(lru_wa, lru_wx),
        gbias=jnp.stack([lru_ba[:, 0], lru_bx[:, 0], lru_ba[:, 1], lru_bx[:, 1]], axis=1),
        lam=lru_lambda,
        lb_logits=hgrn_lb_logits.reshape(DEPTH * 2, D_B),
        hgrn_norm_g=hgrn_norm_g[:, None, :],
        rdl=jnp.broadcast_to(ret_decay_logit.reshape(DEPTH, 2 * RET_HEADS, 1), (DEPTH, 2 * RET_HEADS, TM)),
        ret_norm_g=ret_norm_g[:, None, :],
        w_out=w_out.astype(BF16),
        final_g=final_g[None, :],
    )

    y_prompt, new_lru, new_hgrn, new_ret = _trunk_call(x_prompt, mods, list(range(DEPTH)), False, prm, None, None)

    rope = _rope_tables(x_sample.shape[1])
    z = x_sample
    for l in range(DEPTH):
        (z,) = _trunk_call(z, mods, [l], True, prm, (state_lru, state_hgrn, state_ret), rope)
    y_sample = z

    return (y_prompt, y_sample, new_lru, new_hgrn, new_ret)
```

```python
import functools

import numpy as np

import jax
import jax.numpy as jnp
from jax import lax
from jax.experimental import pallas as pl
from jax.experimental.pallas import tpu as pltpu

F32 = jnp.float32
BF16 = jnp.bfloat16

D_MODEL = 1024
DEPTH = 2
GRID_W = 64
D_A = 512
LRU_BLOCKS = 8
LRU_BW = D_A // LRU_BLOCKS
CONV_W = 4
CONV_LEFT = 1
LRU_C = 8.0
D_B = 512
HG_HEADS = 4
HG_DK = D_B // HG_HEADS
HG_CHUNK = 32
F_MIN = 1e-20
D_C = 512
RET_HEADS = 4
RET_DK = D_C // RET_HEADS
ROPE_BASE = 10000.0
D_MIX = D_A + D_B + D_C
D_IN = 2 * D_A + 5 * D_B + 4 * D_C
EPS = 1e-6

COL_A = 0
COL_B = 2 * D_A
COL_C = COL_B + 5 * D_B

TM = 256
HEAD_W = 128
N_HEADS = 4
GATE_HALF = 256
RET_COL0 = 0
HG_COL0 = D_C
SAFE_LOG_DECAY = 70.0
V7X_VMEM_LIMIT_BYTES = 60000 * 1024
SUBLANES = 8
COND_ROWS = SUBLANES


def _sigmoid(x):
    return 0.5 * jnp.tanh(0.5 * x) + 0.5


def _silu(x):
    h = 0.5 * x
    return h * jnp.tanh(h) + h


def _dot(a, b):
    return jnp.dot(a, b, preferred_element_type=F32)


def _dot_nt(a, b):
    return lax.dot_general(a, b, (((1,), (1,)), ((), ())), preferred_element_type=F32)


def _dot_tn(a, b):
    return lax.dot_general(a, b, (((0,), (0,)), ((), ())), preferred_element_type=F32)


def _loop(n, body, unroll=1):
    if n == 1:
        body(0)
    else:
        def wrapped(i, carry):
            body(i)
            return carry
        lax.fori_loop(0, n, wrapped, 0, unroll=unroll)


def _mod_kernel(c_ref, w_ref, b_ref, o_ref):
    c = c_ref[...]
    o_ref[...] = _dot(_silu(c), w_ref[...]) + b_ref[...]


def _modulation(cond, w_mod, b_mod):
    n_col = 3 * D_MODEL
    blk = D_MODEL
    return pl.pallas_call(
        _mod_kernel,
        out_shape=jax.ShapeDtypeStruct((DEPTH, COND_ROWS, n_col), F32),
        grid=(DEPTH, n_col // blk),
        in_specs=[
            pl.BlockSpec((COND_ROWS, D_MODEL), lambda l, j: (0, 0)),
            pl.BlockSpec((None, D_MODEL, blk), lambda l, j: (l, 0, j)),
            pl.BlockSpec((None, 1, blk), lambda l, j: (l, 0, j)),
        ],
        out_specs=pl.BlockSpec((None, COND_ROWS, blk), lambda l, j: (l, 0, j)),
        name="modulation",
    )(cond, w_mod, b_mod.reshape(DEPTH, 1, n_col))


def _trunk_kernel(*refs, seq, layers, latent, merged):
    n_tiles = seq // TM
    n_chunks = seq // HG_CHUNK
    n_local = len(layers)
    ret_carried = latent or n_tiles > 1
    it = iter(refs)
    x_ref = next(it); mod_ref = next(it); ng_ref = next(it); w_in_ref = next(it)
    cw_ref = next(it); cb_ref = next(it); wblk_ref = next(it); gb_ref = next(it); lam_ref = next(it)
    lbl_ref = next(it); hgn_ref = next(it); rdl_ref = next(it); rtn_ref = next(it)
    w_out_ref = next(it); fg_ref = next(it)
    if latent:
        s_lru_ref = next(it); s_hg_ref = next(it); s_ret_ref = next(it)
        cos_ref = next(it); sina_ref = next(it); sinb_ref = next(it)
    y_ref = next(it)
    if not latent:
        f_lru_ref = next(it); f_hg_ref = next(it); f_ret_ref = next(it)
    xn_s = next(it); ycat_s = next(it); xpad_s = next(it)
    fa_s = next(it); fb_s = next(it); fc_s = next(it); fd_s = next(it)
    hv_s = next(it); hq_s = next(it); hqb_s = next(it); hkd_s = next(it)
    ebl_s = next(it); st_s = next(it)
    ds_s = next(it); wtab_s = next(it); gc_s = next(it); kv_s = next(it)
    xmid_s = next(it) if n_local > 1 else None
    flag_s = next(it)
    if merged:
        hg_if_s = next(it); hg_ib_s = next(it)
        tmp_b_s, tmp_q_s, tmp_k_s = hg_if_s, hg_ib_s, xpad_s
        lru_hf_s, lru_hb_s = fa_s, fc_s
    else:
        lru_hf_s, lru_hb_s = xpad_s, fa_s
        tmp_b_s, tmp_q_s, tmp_k_s = fa_s, fc_s, xpad_s
        hg_if_s, hg_ib_s = fb_s, fd_s

    def rows_of(i):
        return pl.ds(pl.multiple_of(i * TM, TM), TM)

    def head_cols(col0, h):
        return slice(col0 + h * HEAD_W, col0 + (h + 1) * HEAD_W)

    @pl.when(pl.program_id(0) == 0)
    def _():
        ti = lax.broadcasted_iota(jnp.int32, (TM, TM), 0)
        si = lax.broadcasted_iota(jnp.int32, (TM, TM), 1)
        rel = (ti - si).astype(F32)
        tcol = lax.broadcasted_iota(jnp.int32, (TM, HEAD_W), 0).astype(F32)
        for li in range(n_local):
            lg_all = -jax.nn.softplus(-rdl_ref[li][:, :TM])
            for h in range(N_HEADS):
                lg_f = lg_all[h:h + 1, :]
                lg_b = lg_all[N_HEADS + h:N_HEADS + h + 1, :]
                d_f = jnp.where(ti >= si, jnp.exp(jnp.where(ti >= si, rel, 0.0) * lg_f), 0.0)
                d_b = jnp.where(si >= ti, jnp.exp(jnp.where(si >= ti, -rel, 0.0) * lg_b), 0.0)
                ds_s[li, h] = d_f + d_b
                lgf = lg_f[:, :HEAD_W]
                lgb = lg_b[:, :HEAD_W]
                wtab_s[li, 0, h] = jnp.exp((TM - 1.0 - tcol) * lgf)
                wtab_s[li, 1, h] = jnp.exp(tcol * lgb)
                if ret_carried:
                    wtab_s[li, 2, h] = jnp.exp((tcol + 1.0) * lgf)
                    wtab_s[li, 3, h] = jnp.exp((TM - tcol) * lgb)
                gc_s[li, 0, h] = jnp.broadcast_to(jnp.exp(TM * lgf), (8, HEAD_W))
                gc_s[li, 1, h] = jnp.broadcast_to(jnp.exp(TM * lgb), (8, HEAD_W))

    ri = lax.broadcasted_iota(jnp.int32, (TM, TM), 0)
    ci = lax.broadcasted_iota(jnp.int32, (TM, TM), 1)
    same_chunk = (ri // HG_CHUNK) == (ci // HG_CHUNK)
    keeps = (same_chunk & (ci <= ri), same_chunk & (ci >= ri))
    cum_mats = tuple(jnp.where(kp, 1.0, 0.0).astype(BF16) for kp in keeps)
    cpt = TM // HG_CHUNK
    rrow = lax.broadcasted_iota(jnp.int32, (HG_CHUNK, HEAD_W), 0)
    chunk_shape = (cpt, HG_CHUNK, D_B)
    last_row = (HG_CHUNK - 1, 0)
    mid_row = (HG_CHUNK // 2 - 1, HG_CHUNK // 2)
    lbl = lbl_ref[...]

    def run_layer(li, layer, src_ref, dst_ref):
        last = layer == DEPTH - 1
        shift = mod_ref[li, 0:1, :]
        scale = mod_ref[li, 1:2, :]
        gate = mod_ref[li, 2:3, :]

        ng = ng_ref[li, 0:1, :]

        def norm_tile(i):
            rows = rows_of(i)
            x = src_ref[rows, :]
            ms = jnp.mean(x * x, axis=-1, keepdims=True)
            xn = (x * lax.rsqrt(ms + EPS)) * ng
            xn = xn * (1.0 + scale) + shift
            xn_s[rows, :] = xn.astype(BF16)

        def lru_proj_tile(i):
            rows = rows_of(i)
            pa = _dot(xn_s[rows, :], w_in_ref[li, :, COL_A:COL_A + 2 * D_A])
            xpad_s[pl.ds(pl.multiple_of(i * TM + 8, 8), TM), :] = pa[:, :D_A]
            ycat_s[rows, 0:D_A] = _silu(pa[:, D_A:]).astype(BF16)

        cw = cw_ref[li]
        cb = cb_ref[li, 0:1, :]
        nsp = jax.nn.softplus(-lam_ref[li])
        a_refs = (fa_s, fc_s)
        b_refs = (fb_s, fd_s)

        def lru_gate_tile(i):
            rows = rows_of(i)
            win = xpad_s[pl.ds(pl.multiple_of(i * TM, TM), TM + 16), :]
            wn = TM + 16
            u = cb
            for k in range(CONV_W):
                off = k - CONV_LEFT
                sh = win if off == 0 else pltpu.roll(win, (-off) % wn, 0)
                u = u + sh[8:8 + TM, :] * cw[k:k + 1, :]
            for hf in range(D_A // GATE_HALF):
                cols = slice(hf * GATE_HALF, (hf + 1) * GATE_HALF)
                uh = u[:, cols]
                g4 = _dot(uh.astype(BF16), wblk_ref[li, hf])
                for d in range(2):
                    c0 = 2 * d * GATE_HALF
                    r = _sigmoid(g4[:, c0:c0 + GATE_HALF] + gb_ref[li, 2 * d:2 * d + 1, cols])
                    ig = _sigmoid(g4[:, c0 + GATE_HALF:c0 + 2 * GATE_HALF] + gb_ref[li, 2 * d + 1:2 * d + 2, cols])
                    h = jnp.tanh((-0.5 * LRU_C) * r * nsp[d:d + 1, cols])
                    inv = 1.0 / (1.0 - h)
                    nh = -h
                    root = jnp.where(nh > 0.0, nh * lax.rsqrt(nh), 0.0)
                    a_refs[d][rows, cols] = (1.0 + h) * inv
                    b_refs[d][rows, cols] = (2.0 * root * inv) * (ig * uh)

        if latent:
            h0f = s_lru_ref[li, 0:1, :]
            h0b = s_lru_ref[li, 1:2, :]
        else:
            h0f = jnp.zeros((1, D_A), F32)
            h0b = h0f

        def lru_fwd_steps(g, hf):
            tf = pl.multiple_of(g * 8, 8)
            for k in range(8):
                rf = pl.ds(tf + k, 1)
                hf = fa_s[rf, :] * hf + fb_s[rf, :]
                lru_hf_s[rf, :] = hf
            return hf

        def lru_bwd_steps(g, hb):
            tb = pl.multiple_of(seq - 8 - g * 8, 8)
            for k in range(7, -1, -1):
                rb = pl.ds(tb + k, 1)
                hb = fc_s[rb, :] * hb + fd_s[rb, :]
                lru_hb_s[rb, :] = hb
            return hb

        def lru_steps(g, carry):
            return lru_fwd_steps(g, carry[0]), lru_bwd_steps(g, carry[1])

        def lru_finish(carry):
            if not latent:
                f_lru_ref[layer, 0:1, :] = carry[0]
                f_lru_ref[layer, 1:2, :] = carry[1]

        def lru_out_tile(i):
            rows = rows_of(i)
            ya = (lru_hf_s[rows, :] + lru_hb_s[rows, :]) * ycat_s[rows, 0:D_A].astype(F32)
            ycat_s[rows, 0:D_A] = ya.astype(BF16)

        lb_rows = []
        for d in range(2):
            ls = [lbl[k * 2 + d:k * 2 + d + 1, :] for k in range(DEPTH)]
            mx = functools.reduce(jnp.maximum, ls)
            es = [jnp.exp(v - mx) for v in ls]
            den = functools.reduce(lambda p, q: p + q, es)
            acc = jnp.zeros_like(mx)
            for k in range(1, layer + 1):
                acc = acc + es[k] / den
            lb_rows.append(acc)

        def hg_proj(i):
            return _dot(xn_s[rows_of(i), :], w_in_ref[li, :, COL_B:COL_B + 5 * D_B])

        def hg_gates(p, d):
            z = p[:, (1 + d) * D_B:(2 + d) * D_B]
            lb = lb_rows[d]
            sig = _sigmoid(z)
            f = lb + (1.0 - lb) * sig
            logf = jnp.log(jnp.maximum(f, F_MIN))
            k = (1.0 - lb) * (1.0 - sig)
            hi = logf.astype(BF16)
            lo = (logf - hi.astype(F32)).astype(BF16)
            b = _dot(cum_mats[d], hi) + _dot(cum_mats[d], lo)
            return k.reshape(chunk_shape), b.reshape(chunk_shape)

        def hg_tile(i):
            rows = rows_of(i)
            p = hg_proj(i)
            q3 = _silu(p[:, 0:D_B]).reshape(chunk_shape)
            v = p[:, 3 * D_B:4 * D_B].astype(BF16)
            hv_s[rows, :] = v
            ycat_s[rows, D_A:D_A + D_B] = _silu(p[:, 4 * D_B:5 * D_B]).astype(BF16)
            qmb, kib = [], []
            worst = None
            for d in range(2):
                k3, b3 = hg_gates(p, d)
                bl3 = b3[:, last_row[d]:last_row[d] + 1, :]
                m3 = b3[:, mid_row[d]:mid_row[d] + 1, :]
                rel = b3 - m3
                qm = q3 * jnp.exp(rel)
                ki = k3 * jnp.exp(-rel)
                hqb_s[d, rows, :] = (qm * jnp.exp(m3)).reshape(TM, D_B).astype(BF16)
                hkd_s[d, rows, :] = (ki * jnp.exp(bl3 - m3)).reshape(TM, D_B).astype(BF16)
                ebl_s[pl.ds(pl.multiple_of(d * n_chunks + i * cpt, cpt), cpt), :] = jnp.exp(bl3.reshape(cpt, D_B))
                span = jnp.maximum(jnp.max(-m3), jnp.max(m3 - bl3))
                worst = span if worst is None else jnp.maximum(worst, span)
                qmb.append(qm.reshape(TM, D_B).astype(BF16))
                kib.append(ki.reshape(TM, D_B).astype(BF16))
            for h in range(N_HEADS):
                hs = slice(h * HEAD_W, (h + 1) * HEAD_W)
                att = (jnp.where(keeps[0], _dot_nt(qmb[0][:, hs], kib[0][:, hs]), 0.0)
                       + jnp.where(keeps[1], _dot_nt(qmb[1][:, hs], kib[1][:, hs]), 0.0))
                dst_ref[rows, head_cols(HG_COL0, h)] = _dot(att.astype(BF16), v[:, hs])
            return worst > SAFE_LOG_DECAY

        def hg_tile_direct(i):
            rows = rows_of(i)
            p = hg_proj(i)
            q3 = _silu(p[:, 0:D_B]).reshape(chunk_shape)
            tmp_q_s[rows, :] = q3.reshape(TM, D_B)
            for d in range(2):
                k3, b3 = hg_gates(p, d)
                bl3 = b3[:, last_row[d]:last_row[d] + 1, :]
                hqb_s[d, rows, :] = (q3 * jnp.exp(b3)).reshape(TM, D_B).astype(BF16)
                hkd_s[d, rows, :] = (k3 * jnp.exp(bl3 - b3)).reshape(TM, D_B).astype(BF16)
                tmp_b_s[rows, :] = b3.reshape(TM, D_B)
                tmp_k_s[rows, :] = k3.reshape(TM, D_B)

                def chunk(c):
                    rws = pl.ds(pl.multiple_of(i * TM + c * HG_CHUNK, HG_CHUNK), HG_CHUNK)
                    for h in range(N_HEADS):
                        hs = slice(h * HEAD_W, (h + 1) * HEAD_W)
                        bf = tmp_b_s[rws, hs]
                        qf = tmp_q_s[rws, hs]
                        kf = tmp_k_s[rws, hs]
                        vf = hv_s[rws, hs].astype(F32)

                        def src(s, acc):
                            pick = rrow == s
                            ks = jnp.sum(jnp.where(pick, kf, 0.0), axis=0, keepdims=True)
                            vs = jnp.sum(jnp.where(pick, vf, 0.0), axis=0, keepdims=True)
                            bs = jnp.sum(jnp.where(pick, bf, 0.0), axis=0, keepdims=True)
                            w = jnp.sum(qf * ks * jnp.exp(jnp.minimum(bf - bs, 0.0)), axis=-1, keepdims=True)
                            keep = (rrow[:, 0:1] >= s) if d == 0 else (rrow[:, 0:1] <= s)
                            return acc + jnp.where(keep, w, 0.0) * vs

                        o = lax.fori_loop(0, HG_CHUNK, src, jnp.zeros((HG_CHUNK, HEAD_W), F32))
                        cs = head_cols(HG_COL0, h)
                        dst_ref[rws, cs] = o if d == 0 else dst_ref[rws, cs] + o

                _loop(cpt, chunk)

        def hg_state_init():
            for d in range(2):
                for h in range(N_HEADS):
                    if latent:
                        st_s[d * N_HEADS + h] = s_hg_ref[li, d, h].T
                    else:
                        st_s[d * N_HEADS + h] = jnp.zeros((HEAD_W, HEAD_W), F32)

        inter_refs = (hg_if_s, hg_ib_s)

        def hg_scan(j):
            for d in range(2):
                c = j if d == 0 else n_chunks - 1 - j
                rws = pl.ds(pl.multiple_of(c * HG_CHUNK, HG_CHUNK), HG_CHUNK)
                ebl = ebl_s[pl.ds(d * n_chunks + c, 1), :]
                for h in range(N_HEADS):
                    hs = slice(h * HEAD_W, (h + 1) * HEAD_W)
                    st = st_s[d * N_HEADS + h]
                    inter_refs[d][rws, hs] = _dot_nt(hqb_s[d, rws, hs], st.astype(BF16))
                    st_s[d * N_HEADS + h] = st * ebl[:, hs] + _dot_tn(hv_s[rws, hs], hkd_s[d, rws, hs])

        def hg_finish():
            if not latent:
                for d in range(2):
                    for h in range(N_HEADS):
                        f_hg_ref[layer, d, h] = st_s[d * N_HEADS + h].T

        def head_norm_gate(y, gain, col0, rows):
            for h in range(N_HEADS):
                hs = slice(h * HEAD_W, (h + 1) * HEAD_W)
                yh = y[:, hs]
                ms = jnp.mean(yh * yh, axis=-1, keepdims=True)
                yn = yh * lax.rsqrt(ms + EPS) * gain[:, hs]
                cs = slice(col0 + h * HEAD_W, col0 + (h + 1) * HEAD_W)
                ycat_s[rows, cs] = (yn * ycat_s[rows, cs].astype(F32)).astype(BF16)

        hgn = hgn_ref[li, 0:1, :]

        def hg_out_tile(i):
            rows = rows_of(i)
            y = dst_ref[rows, HG_COL0:HG_COL0 + D_B] + hg_if_s[rows, :] + hg_ib_s[rows, :]
            head_norm_gate(y, hgn, D_A, rows)

        def ret_tile(i):
            rows = rows_of(i)
            p = _dot(xn_s[rows, :], w_in_ref[li, :, COL_C:COL_C + 4 * D_C])
            ycat_s[rows, D_A + D_B:D_MIX] = _silu(p[:, 3 * D_C:4 * D_C]).astype(BF16)
            for h in range(N_HEADS):
                hs = slice(h * HEAD_W, (h + 1) * HEAD_W)
                qh = p[:, hs]
                kh = p[:, D_C + h * HEAD_W:D_C + (h + 1) * HEAD_W]
                vh = p[:, 2 * D_C + h * HEAD_W:2 * D_C + (h + 1) * HEAD_W].astype(BF16)
                if latent:
                    cos = cos_ref[rows, :]
                    sa = sina_ref[rows, :]
                    sb = sinb_ref[rows, :]
                    qh = qh * cos + pltpu.roll(qh, HEAD_W - 32, 1) * sa + pltpu.roll(qh, 32, 1) * sb
                    kh = kh * cos + pltpu.roll(kh, HEAD_W - 32, 1) * sa + pltpu.roll(kh, 32, 1) * sb
                qh = qh * (RET_DK ** -0.5)
                qb = qh.astype(BF16)
                hq_s[rows, hs] = qb
                sc = _dot_nt(qb, kh.astype(BF16)) * ds_s[li, h]
                dst_ref[rows, head_cols(RET_COL0, h)] = _dot(sc.astype(BF16), vh)
                for d in range(2):
                    kw = (kh * wtab_s[li, d, h]).astype(BF16)
                    kv_s[(i * 2 + d) * N_HEADS + h] = _dot_tn(kw, vh)

        def ret_carry():
            for d in range(2):
                for h in range(N_HEADS):
                    hs = slice(h * HEAD_W, (h + 1) * HEAD_W)
                    if latent:
                        state = s_ret_ref[li, d, h]
                    else:
                        state = jnp.zeros((HEAD_W, HEAD_W), F32)
                    g_c = gc_s[li, d, h][0:1, :]
                    order = range(n_tiles) if d == 0 else range(n_tiles - 1, -1, -1)
                    for n in order:
                        rows = pl.ds(n * TM, TM)
                        if ret_carried:
                            qw =(hq_s[rows, hs].astype(F32) * wtab_s[li, 2 + d, h]).astype(BF16)
                            cs = head_cols(RET_COL0, h)
                            dst_ref[rows, cs] = dst_ref[rows, cs] + _dot(qw, state.astype(BF16))
                        state = state * g_c + kv_s[(n * 2 + d) * N_HEADS + h]
                    if not latent:
                        f_ret_ref[layer, d, h] = state

        rtn = rtn_ref[li, 0:1, :]

        def ret_out_tile(i):
            rows = rows_of(i)
            head_norm_gate(dst_ref[rows, RET_COL0:RET_COL0 + D_C], rtn, D_A + D_B, rows)

        fg = fg_ref[0:1, :]

        def out_tile(i):
            rows = rows_of(i)
            delta = _dot(ycat_s[rows, :], w_out_ref[li])
            out = src_ref[rows, :] + gate * delta
            if last:
                ms = jnp.mean(out * out, axis=-1, keepdims=True)
                out = out * lax.rsqrt(ms + EPS) * fg
            dst_ref[rows, :] = out

        zero_pad = jnp.zeros((8, D_A), F32)
        xpad_s[pl.ds(0, 8), :] = zero_pad
        xpad_s[pl.ds(seq + 8, 8), :] = zero_pad

        def redo_tile_if(unsafe, i):
            @pl.when(unsafe)
            def _():
                hg_tile_direct(i)

        def front_tile(i):
            norm_tile(i)
            lru_proj_tile(i)

        def mixer_tile(i):
            flag_s[i] = hg_tile(i).astype(jnp.int32)
            ret_tile(i)
            lru_gate_tile(i)

        def redo_tile(i):
            redo_tile_if(flag_s[i] == 1, i)

        def scan_chunk(j, carry):
            hg_scan(j)
            for g in range(HG_CHUNK // 8):
                carry = lru_steps(j * (HG_CHUNK // 8) + g, carry)
            return carry

        def back_tile(i):
            if merged:
                lru_out_tile(i)
            hg_out_tile(i)
            ret_out_tile(i)
            out_tile(i)

        _loop(n_tiles, front_tile)
        _loop(n_tiles, mixer_tile)
        if merged:
            _loop(n_tiles, redo_tile)
            hg_state_init()
            lru_finish(lax.fori_loop(0, n_chunks, scan_chunk, (h0f, h0b), unroll=n_chunks))
        else:
            hf_fin = lax.fori_loop(0, seq // 8, lru_fwd_steps, h0f)
            hb_fin = lax.fori_loop(0, seq // 8, lru_bwd_steps, h0b)
            lru_finish((hf_fin, hb_fin))
            _loop(n_tiles, lru_out_tile)
            _loop(n_tiles, redo_tile)
            hg_state_init()
            _loop(n_chunks, hg_scan, unroll=8)
        hg_finish()
        ret_carry()
        _loop(n_tiles, back_tile)

    for li, layer in enumerate(layers):
        src = x_ref if li == 0 else xmid_s
        dst = y_ref if li == n_local - 1 else xmid_s
        run_layer(li, layer, src, dst)


def _trunk_call(x, mods, layers, latent, prm, states, rope):
    bsz, seq, _ = x.shape
    n_tiles = seq // TM
    n_chunks = seq // HG_CHUNK
    n_local = len(layers)
    l0 = layers[0]
    state_tail = (2, N_HEADS, HEAD_W, HEAD_W)

    def layers_spec(arr):
        tail = arr.shape[1:]
        zeros = (0,) * len(tail)
        return pl.BlockSpec((n_local,) + tail, lambda b: (l0 // n_local,) + zeros, pipeline_mode=pl.Buffered(1))

    def whole_spec(arr):
        zeros = (0,) * arr.ndim
        return pl.BlockSpec(arr.shape, lambda b: zeros, pipeline_mode=pl.Buffered(1))

    def per_seq_layers(tail, n_lay, first):
        zeros = (0,) * len(tail)
        return pl.BlockSpec((None, n_lay) + tail, lambda b: (b, first // n_lay) + zeros)

    act_mode = dict(pipeline_mode=pl.Buffered(1)) if latent else {}
    act_spec = pl.BlockSpec((None, seq, D_MODEL), lambda b: (b, 0, 0), **act_mode)
    cond_row = (lambda b: (l0 // n_local, 1 + b, 0, 0)) if latent else (lambda b: (l0 // n_local, 0, 0, 0))
    mod_spec = pl.BlockSpec((n_local, None, SUBLANES, D_MODEL), cond_row)

    stacked = [prm[k] for k in ("norm_g", "w_in", "conv_w", "conv_b", "wblk", "gbias", "lam")]
    operands = [x, mods] + stacked + [prm["lb_logits"], prm["hgrn_norm_g"], prm["rdl"], prm["ret_norm_g"],
                                      prm["w_out"], prm["final_g"]]
    in_specs = ([act_spec, mod_spec] + [layers_spec(a) for a in stacked]
                + [whole_spec(prm["lb_logits"]), layers_spec(prm["hgrn_norm_g"]), layers_spec(prm["rdl"]),
                   layers_spec(prm["ret_norm_g"]), layers_spec(prm["w_out"]), whole_spec(prm["final_g"])])
    if latent:
        operands += list(states) + list(rope)
        in_specs += [per_seq_layers((SUBLANES, D_A), n_local, l0), per_seq_layers(state_tail, n_local, l0),
                     per_seq_layers(state_tail, n_local, l0)]
        in_specs += [whole_spec(r) for r in rope]
    out_shape = [jax.ShapeDtypeStruct((bsz, seq, D_MODEL), F32)]
    out_specs = [act_spec]
    if not latent:
        assert n_local == DEPTH, "the state outputs are written whole, one call must cover every layer"
        out_shape += [jax.ShapeDtypeStruct((bsz, DEPTH, 2, D_A), F32),
                      jax.ShapeDtypeStruct((bsz, DEPTH) + state_tail, F32),
                      jax.ShapeDtypeStruct((bsz, DEPTH) + state_tail, F32)]
        out_specs += [per_seq_layers((2, D_A), DEPTH, 0), per_seq_layers(state_tail, DEPTH, 0),
                      per_seq_layers(state_tail, DEPTH, 0)]
    scratch = [
        pltpu.VMEM((seq, D_MODEL), BF16),
        pltpu.VMEM((seq, D_MIX), BF16),
        pltpu.VMEM((seq + 16, D_A), F32),
        pltpu.VMEM((seq, D_A), F32), pltpu.VMEM((seq, D_A), F32),
        pltpu.VMEM((seq, D_A), F32), pltpu.VMEM((seq, D_A), F32),
        pltpu.VMEM((seq, D_B), BF16), pltpu.VMEM((seq, D_B), BF16),
        pltpu.VMEM((2, seq, D_B), BF16), pltpu.VMEM((2, seq, D_B), BF16),
        pltpu.VMEM((2 * n_chunks, D_B), F32),
        pltpu.VMEM((2 * N_HEADS, HEAD_W, HEAD_W), F32),
        pltpu.VMEM((n_local, N_HEADS, TM, TM), F32),
        pltpu.VMEM((n_local, 4 if (latent or n_tiles > 1) else 2, N_HEADS, TM, HEAD_W), F32),
        pltpu.VMEM((n_local, 2, N_HEADS, 8, HEAD_W), F32),
        pltpu.VMEM((n_tiles * 2 * N_HEADS, HEAD_W, HEAD_W), F32),
    ]
    if n_local > 1:
        scratch.append(pltpu.VMEM((seq, D_MODEL), F32))
    scratch += [pltpu.SMEM((n_tiles,), jnp.int32)]
    merged = n_chunks <= 8
    if merged:
        scratch += [pltpu.VMEM((seq, D_B), F32)] * 2
    body = functools.partial(_trunk_kernel, seq=seq, layers=tuple(layers), latent=latent, merged=merged)
    return pl.pallas_call(
        body,
        out_shape=out_shape,
        grid=(bsz,),
        in_specs=in_specs,
        out_specs=out_specs,
        scratch_shapes=scratch,
        compiler_params=pltpu.CompilerParams(dimension_semantics=("arbitrary",),
                                             vmem_limit_bytes=V7X_VMEM_LIMIT_BYTES),
        name=("latent" if latent else "context") + "_layers" + "".join(str(l) for l in layers),
    )(*operands)


def _rope_tables(seq):
    half = RET_DK // 2
    nf = half // 2
    t = np.arange(seq)
    freq = (ROPE_BASE ** (-np.arange(nf, dtype=np.float32) / nf)).astype(np.float32)
    ang_r = (t // GRID_W).astype(np.float32)[:, None] * freq[None, :]
    ang_c = (t % GRID_W).astype(np.float32)[:, None] * freq[None, :]
    zero = np.zeros((seq, nf), np.float32)
    cos = np.concatenate([np.cos(ang_r)] * 2 + [np.cos(ang_c)] * 2, axis=-1)
    sin_a = np.concatenate([-np.sin(ang_r), zero, -np.sin(ang_c), zero], axis=-1)
    sin_b = np.concatenate([zero, np.sin(ang_r), zero, np.sin(ang_c)], axis=-1)
    return tuple(jnp.asarray(a, F32) for a in (cos, sin_a, sin_b))


def _block_diag_gates(wa, wx):
    per_half = GATE_HALF // LRU_BW
    n_half = D_A // GATE_HALF
    eye = jnp.eye(per_half, dtype=F32)
    cols = []
    for d in range(2):
        for w in (wa, wx):
            blk = w[:, d].reshape(DEPTH, n_half, per_half, LRU_BW, LRU_BW)
            dense = jnp.einsum('pq,lhpij->lhpiqj', eye, blk).reshape(DEPTH, n_half, GATE_HALF, GATE_HALF)
            cols.append(dense)
    return jnp.concatenate(cols, axis=-1).astype(BF16)


def kernel(x_prompt, x_sample, state_lru, state_hgrn, state_ret, c, c_ctx, norm_g, w_mod, b_mod, w_in, conv_w,
           conv_b, lru_wa, lru_ba, lru_wx, lru_bx, lru_lambda, hgrn_lb_logits, hgrn_norm_g, ret_decay_logit,
           ret_norm_g, w_out, final_g):
    dec_b = x_sample.shape[0]
    pad = jnp.zeros((COND_ROWS - 1 - dec_b, D_MODEL), F32)
    cond = jnp.concatenate([c_ctx[None, :], c, pad], axis=0)
    def rows8(a):
        widths = [(0, 0)] * a.ndim
        widths[-2] = (0, SUBLANES - a.shape[-2])
        return jnp.pad(a, widths)

    mods = rows8(_modulation(cond, w_mod, b_mod).reshape(DEPTH, COND_ROWS, 3, D_MODEL))

    prm = dict(
        norm_g=rows8(norm_g[:, None, :]),
        w_in=w_in.astype(BF16),
        conv_w=rows8(conv_w),
        conv_b=rows8(conv_b[:, None, :]),
        wblk=_block_diag_gates(lru_wa, lru_wx),
        gbias=rows8(jnp.stack([lru_ba[:, 0], lru_bx[:, 0], lru_ba[:, 1], lru_bx[:, 1]], axis=1)),
        lam=rows8(lru_lambda),
        lb_logits=rows8(hgrn_lb_logits.reshape(DEPTH * 2, D_B)),
        hgrn_norm_g=rows8(hgrn_norm_g[:, None, :]),
        rdl=jnp.broadcast_to(ret_decay_logit.reshape(DEPTH, 2 * RET_HEADS, 1), (DEPTH, 2 * RET_HEADS, 2 * TM)),
        ret_norm_g=rows8(ret_norm_g[:, None, :]),
        w_out=w_out.astype(BF16),
        final_g=rows8(final_g[None, :]),
    )

    y_prompt, new_lru, new_hgrn, new_ret = _trunk_call(x_prompt, mods, list(range(DEPTH)), False, prm, None, None)

    rope = _rope_tables(x_sample.shape[1])
    z = x_sample
    for l in range(DEPTH):
        (z,) = _trunk_call(z, mods, [l], True, prm, (rows8(state_lru), state_hgrn, state_ret), rope)
    y_sample = z

    return (y_prompt, y_sample, new_lru, new_hgrn, new_ret)
```

```python
import functools

import numpy as np

import jax
import jax.numpy as jnp
from jax import lax
from jax.experimental import pallas as pl
from jax.experimental.pallas import tpu as pltpu

F32 = jnp.float32
BF16 = jnp.bfloat16

D_MODEL = 1024
DEPTH = 2
GRID_W = 64
D_A = 512
LRU_BLOCKS = 8
LRU_BW = D_A // LRU_BLOCKS
CONV_W = 4
CONV_LEFT = 1
LRU_C = 8.0
D_B = 512
HG_HEADS = 4
HG_DK = D_B // HG_HEADS
HG_CHUNK = 32
F_MIN = 1e-20
D_C = 512
RET_HEADS = 4
RET_DK = D_C // RET_HEADS
ROPE_BASE = 10000.0
D_MIX = D_A + D_B + D_C
D_IN = 2 * D_A + 5 * D_B + 4 * D_C
EPS = 1e-6

COL_A = 0
COL_B = 2 * D_A
COL_C = COL_B + 5 * D_B

TM = 256
HEAD_W = 128
N_HEADS = 4
GATE_HALF = 256
RET_COL0 = 0
HG_COL0 = D_C
SAFE_LOG_DECAY = 70.0
V7X_VMEM_LIMIT_BYTES = 60000 * 1024
COND_ROWS = 8


def _sigmoid(x):
    return 0.5 * jnp.tanh(0.5 * x) + 0.5


def _silu(x):
    h = 0.5 * x
    return h * jnp.tanh(h) + h


def _dot(a, b):
    return jnp.dot(a, b, preferred_element_type=F32)


def _dot_nt(a, b):
    return lax.dot_general(a, b, (((1,), (1,)), ((), ())), preferred_element_type=F32)


def _dot_tn(a, b):
    return lax.dot_general(a, b, (((0,), (0,)), ((), ())), preferred_element_type=F32)


def _loop(n, body, unroll=1):
    if n == 1:
        body(0)
    else:
        def wrapped(i, carry):
            body(i)
            return carry
        lax.fori_loop(0, n, wrapped, 0, unroll=unroll)


def _mod_kernel(c_ref, w_ref, b_ref, o_ref):
    c = c_ref[...]
    o_ref[...] = _dot(_silu(c), w_ref[...]) + b_ref[...]


def _modulation(cond, w_mod, b_mod):
    n_col = 3 * D_MODEL
    blk = D_MODEL
    return pl.pallas_call(
        _mod_kernel,
        out_shape=jax.ShapeDtypeStruct((DEPTH, COND_ROWS, n_col), F32),
        grid=(DEPTH, n_col // blk),
        in_specs=[
            pl.BlockSpec((COND_ROWS, D_MODEL), lambda l, j: (0, 0)),
            pl.BlockSpec((None, D_MODEL, blk), lambda l, j: (l, 0, j)),
            pl.BlockSpec((None, 1, blk), lambda l, j: (l, 0, j)),
        ],
        out_specs=pl.BlockSpec((None, COND_ROWS, blk), lambda l, j: (l, 0, j)),
        name="modulation",
    )(cond, w_mod, b_mod.reshape(DEPTH, 1, n_col))


def _trunk_kernel(*refs, seq, layers, latent, merged):
    n_tiles = seq // TM
    n_chunks = seq // HG_CHUNK
    n_local = len(layers)
    ret_carried = latent or n_tiles > 1
    it = iter(refs)
    x_ref = next(it); mod_ref = next(it); ng_ref = next(it); w_in_ref = next(it)
    cw_ref = next(it); cb_ref = next(it); wblk_ref = next(it); gb_ref = next(it); lam_ref = next(it)
    lbl_ref = next(it); hgn_ref = next(it); rdl_ref = next(it); rtn_ref = next(it)
    w_out_ref = next(it); fg_ref = next(it)
    if latent:
        s_lru_ref = next(it); s_hg_ref = next(it); s_ret_ref = next(it)
        cos_ref = next(it); sina_ref = next(it); sinb_ref = next(it)
    y_ref = next(it)
    if not latent:
        f_lru_ref = next(it); f_hg_ref = next(it); f_ret_ref = next(it)
    xn_s = next(it); ycat_s = next(it); xpad_s = next(it)
    fa_s = next(it); fb_s = next(it); fc_s = next(it); fd_s = next(it)
    hv_s = next(it); hq_s = next(it); hqb_s = next(it); hkd_s = next(it)
    ebl_s = next(it); st_s = next(it)
    ds_s = next(it); wtab_s = next(it); gc_s = next(it); kv_s = next(it)
    xmid_s = next(it) if n_local > 1 else None
    flag_s = next(it)
    if merged:
        hg_if_s = next(it); hg_ib_s = next(it)
        tmp_b_s, tmp_q_s, tmp_k_s = hg_if_s, hg_ib_s, xpad_s
        lru_hf_s, lru_hb_s = fa_s, fc_s
    else:
        lru_hf_s, lru_hb_s = xpad_s, fa_s
        tmp_b_s, tmp_q_s, tmp_k_s = fa_s, fc_s, xpad_s
        hg_if_s, hg_ib_s = fb_s, fd_s

    def rows_of(i):
        return pl.ds(pl.multiple_of(i * TM, TM), TM)

    def head_cols(col0, h):
        return slice(col0 + h * HEAD_W, col0 + (h + 1) * HEAD_W)

    @pl.when(pl.program_id(0) == 0)
    def _():
        ti = lax.broadcasted_iota(jnp.int32, (TM, TM), 0)
        si = lax.broadcasted_iota(jnp.int32, (TM, TM), 1)
        rel = (ti - si).astype(F32)
        tcol = lax.broadcasted_iota(jnp.int32, (TM, HEAD_W), 0).astype(F32)
        for li in range(n_local):
            lg_all = -jax.nn.softplus(-rdl_ref[li])
            for h in range(N_HEADS):
                lg_f = lg_all[h:h + 1, :]
                lg_b = lg_all[N_HEADS + h:N_HEADS + h + 1, :]
                d_f = jnp.where(ti >= si, jnp.exp(jnp.where(ti >= si, rel, 0.0) * lg_f), 0.0)
                d_b = jnp.where(si >= ti, jnp.exp(jnp.where(si >= ti, -rel, 0.0) * lg_b), 0.0)
                ds_s[li, h] = d_f + d_b
                lgf = lg_f[:, :HEAD_W]
                lgb = lg_b[:, :HEAD_W]
                wtab_s[li, 0, h] = jnp.exp((TM - 1.0 - tcol) * lgf)
                wtab_s[li, 1, h] = jnp.exp(tcol * lgb)
                if ret_carried:
                    wtab_s[li, 2, h] = jnp.exp((tcol + 1.0) * lgf)
                    wtab_s[li, 3, h] = jnp.exp((TM - tcol) * lgb)
                gc_s[li, 0, h] = jnp.broadcast_to(jnp.exp(TM * lgf), (8, HEAD_W))
                gc_s[li, 1, h] = jnp.broadcast_to(jnp.exp(TM * lgb), (8, HEAD_W))

    ri = lax.broadcasted_iota(jnp.int32, (TM, TM), 0)
    ci = lax.broadcasted_iota(jnp.int32, (TM, TM), 1)
    same_chunk = (ri // HG_CHUNK) == (ci // HG_CHUNK)
    keeps = (same_chunk & (ci <= ri), same_chunk & (ci >= ri))
    cum_mats = tuple(jnp.where(kp, 1.0, 0.0).astype(BF16) for kp in keeps)
    cpt = TM // HG_CHUNK
    rrow = lax.broadcasted_iota(jnp.int32, (HG_CHUNK, HEAD_W), 0)
    chunk_shape = (cpt, HG_CHUNK, D_B)
    last_row = (HG_CHUNK - 1, 0)
    mid_row = (HG_CHUNK // 2 - 1, HG_CHUNK // 2)
    lbl = lbl_ref[...]

    def run_layer(li, layer, src_ref, dst_ref):
        last = layer == DEPTH - 1
        shift = mod_ref[li, 0:1, :]
        scale = mod_ref[li, 1:2, :]
        gate = mod_ref[li, 2:3, :]

        ng = ng_ref[li]

        def norm_tile(i):
            rows = rows_of(i)
            x = src_ref[rows, :]
            ms = jnp.mean(x * x, axis=-1, keepdims=True)
            xn = (x * lax.rsqrt(ms + EPS)) * ng
            xn = xn * (1.0 + scale) + shift
            xn_s[rows, :] = xn.astype(BF16)

        def lru_proj_tile(i):
            rows = rows_of(i)
            pa = _dot(xn_s[rows, :], w_in_ref[li, :, COL_A:COL_A + 2 * D_A])
            xpad_s[pl.ds(pl.multiple_of(i * TM + 8, 8), TM), :] = pa[:, :D_A]
            ycat_s[rows, 0:D_A] = _silu(pa[:, D_A:]).astype(BF16)

        cw = cw_ref[li]
        cb = cb_ref[li]
        nsp = jax.nn.softplus(-lam_ref[li])
        a_refs = (fa_s, fc_s)
        b_refs = (fb_s, fd_s)

        def lru_gate_tile(i):
            rows = rows_of(i)
            win = xpad_s[pl.ds(pl.multiple_of(i * TM, TM), TM + 16), :]
            wn = TM + 16
            u = cb
            for k in range(CONV_W):
                off = k - CONV_LEFT
                sh = win if off == 0 else pltpu.roll(win, (-off) % wn, 0)
                u = u + sh[8:8 + TM, :] * cw[k:k + 1, :]
            for hf in range(D_A // GATE_HALF):
                cols = slice(hf * GATE_HALF, (hf + 1) * GATE_HALF)
                uh = u[:, cols]
                g4 = _dot(uh.astype(BF16), wblk_ref[li, hf])
                for d in range(2):
                    c0 = 2 * d * GATE_HALF
                    r = _sigmoid(g4[:, c0:c0 + GATE_HALF] + gb_ref[li, 2 * d:2 * d + 1, cols])
                    ig = _sigmoid(g4[:, c0 + GATE_HALF:c0 + 2 * GATE_HALF] + gb_ref[li, 2 * d + 1:2 * d + 2, cols])
                    h = jnp.tanh((-0.5 * LRU_C) * r * nsp[d:d + 1, cols])
                    inv = 1.0 / (1.0 - h)
                    nh = -h
                    root = jnp.where(nh > 0.0, nh * lax.rsqrt(nh), 0.0)
                    a_refs[d][rows, cols] = (1.0 + h) * inv
                    b_refs[d][rows, cols] = (2.0 * root * inv) * (ig * uh)

        if latent:
            h0f = s_lru_ref[li, 0:1, :]
            h0b = s_lru_ref[li, 1:2, :]
        else:
            h0f = jnp.zeros((1, D_A), F32)
            h0b = h0f

        def lru_fwd_steps(g, hf):
            tf = pl.multiple_of(g * 8, 8)
            for k in range(8):
                rf = pl.ds(tf + k, 1)
                hf = fa_s[rf, :] * hf + fb_s[rf, :]
                lru_hf_s[rf, :] = hf
            return hf

        def lru_bwd_steps(g, hb):
            tb = pl.multiple_of(seq - 8 - g * 8, 8)
            for k in range(7, -1, -1):
                rb = pl.ds(tb + k, 1)
                hb = fc_s[rb, :] * hb + fd_s[rb, :]
                lru_hb_s[rb, :] = hb
            return hb

        def lru_steps(g, carry):
            return lru_fwd_steps(g, carry[0]), lru_bwd_steps(g, carry[1])

        def lru_finish(carry):
            if not latent:
                f_lru_ref[layer, 0:1, :] = carry[0]
                f_lru_ref[layer, 1:2, :] = carry[1]

        def lru_out_tile(i):
            rows = rows_of(i)
            ya = (lru_hf_s[rows, :] + lru_hb_s[rows, :]) * ycat_s[rows, 0:D_A].astype(F32)
            ycat_s[rows, 0:D_A] = ya.astype(BF16)

        lb_rows = []
        for d in range(2):
            ls = [lbl[k * 2 + d:k * 2 + d + 1, :] for k in range(DEPTH)]
            mx = functools.reduce(jnp.maximum, ls)
            es = [jnp.exp(v - mx) for v in ls]
            den = functools.reduce(lambda p, q: p + q, es)
            acc = jnp.zeros_like(mx)
            for k in range(1, layer + 1):
                acc = acc + es[k] / den
            lb_rows.append(acc)

        def hg_proj(i):
            return _dot(xn_s[rows_of(i), :], w_in_ref[li, :, COL_B:COL_B + 5 * D_B])

        def hg_gates(p, d):
            z = p[:, (1 + d) * D_B:(2 + d) * D_B]
            lb = lb_rows[d]
            sig = _sigmoid(z)
            f = lb + (1.0 - lb) * sig
            logf = jnp.log(jnp.maximum(f, F_MIN))
            k = (1.0 - lb) * (1.0 - sig)
            hi = logf.astype(BF16)
            lo = (logf - hi.astype(F32)).astype(BF16)
            b = _dot(cum_mats[d], hi) + _dot(cum_mats[d], lo)
            return k.reshape(chunk_shape), b.reshape(chunk_shape)

        def hg_tile(i):
            rows = rows_of(i)
            p = hg_proj(i)
            q3 = _silu(p[:, 0:D_B]).reshape(chunk_shape)
            v = p[:, 3 * D_B:4 * D_B].astype(BF16)
            hv_s[rows, :] = v
            ycat_s[rows, D_A:D_A + D_B] = _silu(p[:, 4 * D_B:5 * D_B]).astype(BF16)
            qmb, kib = [], []
            worst = None
            for d in range(2):
                k3, b3 = hg_gates(p, d)
                bl3 = b3[:, last_row[d]:last_row[d] + 1, :]
                m3 = b3[:, mid_row[d]:mid_row[d] + 1, :]
                rel = b3 - m3
                qm = q3 * jnp.exp(rel)
                ki = k3 * jnp.exp(-rel)
                hqb_s[d, rows, :] = (qm * jnp.exp(m3)).reshape(TM, D_B).astype(BF16)
                hkd_s[d, rows, :] = (ki * jnp.exp(bl3 - m3)).reshape(TM, D_B).astype(BF16)
                ebl_s[pl.ds(pl.multiple_of(d * n_chunks + i * cpt, cpt), cpt), :] = jnp.exp(bl3.reshape(cpt, D_B))
                span = jnp.maximum(jnp.max(-m3), jnp.max(m3 - bl3))
                worst = span if worst is None else jnp.maximum(worst, span)
                qmb.append(qm.reshape(TM, D_B).astype(BF16))
                kib.append(ki.reshape(TM, D_B).astype(BF16))
            for h in range(N_HEADS):
                hs = slice(h * HEAD_W, (h + 1) * HEAD_W)
                att = (jnp.where(keeps[0], _dot_nt(qmb[0][:, hs], kib[0][:, hs]), 0.0)
                       + jnp.where(keeps[1], _dot_nt(qmb[1][:, hs], kib[1][:, hs]), 0.0))
                dst_ref[rows, head_cols(HG_COL0, h)] = _dot(att.astype(BF16), v[:, hs])
            return worst > SAFE_LOG_DECAY

        def hg_tile_direct(i):
            rows = rows_of(i)
            p = hg_proj(i)
            q3 = _silu(p[:, 0:D_B]).reshape(chunk_shape)
            tmp_q_s[rows, :] = q3.reshape(TM, D_B)
            for d in range(2):
                k3, b3 = hg_gates(p, d)
                bl3 = b3[:, last_row[d]:last_row[d] + 1, :]
                hqb_s[d, rows, :] = (q3 * jnp.exp(b3)).reshape(TM, D_B).astype(BF16)
                hkd_s[d, rows, :] = (k3 * jnp.exp(bl3 - b3)).reshape(TM, D_B).astype(BF16)
                tmp_b_s[rows, :] = b3.reshape(TM, D_B)
                tmp_k_s[rows, :] = k3.reshape(TM, D_B)

                def chunk(c):
                    rws = pl.ds(pl.multiple_of(i * TM + c * HG_CHUNK, HG_CHUNK), HG_CHUNK)
                    for h in range(N_HEADS):
                        hs = slice(h * HEAD_W, (h + 1) * HEAD_W)
                        bf = tmp_b_s[rws, hs]
                        qf = tmp_q_s[rws, hs]
                        kf = tmp_k_s[rws, hs]
                        vf = hv_s[rws, hs].astype(F32)

                        def src(s, acc):
                            pick = rrow == s
                            ks = jnp.sum(jnp.where(pick, kf, 0.0), axis=0, keepdims=True)
                            vs = jnp.sum(jnp.where(pick, vf, 0.0), axis=0, keepdims=True)
                            bs = jnp.sum(jnp.where(pick, bf, 0.0), axis=0, keepdims=True)
                            w = jnp.sum(qf * ks * jnp.exp(jnp.minimum(bf - bs, 0.0)), axis=-1, keepdims=True)
                            keep = (rrow[:, 0:1] >= s) if d == 0 else (rrow[:, 0:1] <= s)
                            return acc + jnp.where(keep, w, 0.0) * vs

                        o = lax.fori_loop(0, HG_CHUNK, src, jnp.zeros((HG_CHUNK, HEAD_W), F32))
                        cs = head_cols(HG_COL0, h)
                        dst_ref[rws, cs] = o if d == 0 else dst_ref[rws, cs] + o

                _loop(cpt, chunk)

        def hg_state_init():
            for d in range(2):
                for h in range(N_HEADS):
                    if latent:
                        st_s[d * N_HEADS + h] = s_hg_ref[li, d, h].T
                    else:
                        st_s[d * N_HEADS + h] = jnp.zeros((HEAD_W, HEAD_W), F32)

        inter_refs = (hg_if_s, hg_ib_s)

        def hg_scan(j):
            for d in range(2):
                c = j if d == 0 else n_chunks - 1 - j
                rws = pl.ds(pl.multiple_of(c * HG_CHUNK, HG_CHUNK), HG_CHUNK)
                ebl = ebl_s[pl.ds(d * n_chunks + c, 1), :]
                for h in range(N_HEADS):
                    hs = slice(h * HEAD_W, (h + 1) * HEAD_W)
                    kv = _dot_tn(hv_s[rws, hs], hkd_s[d, rws, hs])
                    if isinstance(j, int) and j == 0 and not latent:
                        inter_refs[d][rws, hs] = jnp.zeros((HG_CHUNK, HEAD_W), F32)
                        st_s[d * N_HEADS + h] = kv
                    else:
                        st = st_s[d * N_HEADS + h]
                        inter_refs[d][rws, hs] = _dot_nt(hqb_s[d, rws, hs], st.astype(BF16))
                        st_s[d * N_HEADS + h] = st * ebl[:, hs] + kv

        def hg_finish():
            if not latent:
                for d in range(2):
                    for h in range(N_HEADS):
                        f_hg_ref[layer, d, h] = st_s[d * N_HEADS + h].T

        def head_norm_gate(y, gain, col0, rows):
            for h in range(N_HEADS):
                hs = slice(h * HEAD_W, (h + 1) * HEAD_W)
                yh = y[:, hs]
                ms = jnp.mean(yh * yh, axis=-1, keepdims=True)
                yn = yh * lax.rsqrt(ms + EPS) * gain[:, hs]
                cs = slice(col0 + h * HEAD_W, col0 + (h + 1) * HEAD_W)
                ycat_s[rows, cs] = (yn * ycat_s[rows, cs].astype(F32)).astype(BF16)

        hgn = hgn_ref[li]

        def hg_out_tile(i):
            rows = rows_of(i)
            y = dst_ref[rows, HG_COL0:HG_COL0 + D_B] + hg_if_s[rows, :] + hg_ib_s[rows, :]
            head_norm_gate(y, hgn, D_A, rows)

        def ret_tile(i):
            rows = rows_of(i)
            p = _dot(xn_s[rows, :], w_in_ref[li, :, COL_C:COL_C + 4 * D_C])
            ycat_s[rows, D_A + D_B:D_MIX] = _silu(p[:, 3 * D_C:4 * D_C]).astype(BF16)
            for h in range(N_HEADS):
                hs = slice(h * HEAD_W, (h + 1) * HEAD_W)
                qh = p[:, hs]
                kh = p[:, D_C + h * HEAD_W:D_C + (h + 1) * HEAD_W]
                vh = p[:, 2 * D_C + h * HEAD_W:2 * D_C + (h + 1) * HEAD_W].astype(BF16)
                if latent:
                    cos = cos_ref[rows, :]
                    sa = sina_ref[rows, :]
                    sb = sinb_ref[rows, :]
                    qh = qh * cos + pltpu.roll(qh, HEAD_W - 32, 1) * sa + pltpu.roll(qh, 32, 1) * sb
                    kh = kh * cos + pltpu.roll(kh, HEAD_W - 32, 1) * sa + pltpu.roll(kh, 32, 1) * sb
                qh = qh * (RET_DK ** -0.5)
                qb = qh.astype(BF16)
                hq_s[rows, hs] = qb
                sc = _dot_nt(qb, kh.astype(BF16)) * ds_s[li, h]
                dst_ref[rows, head_cols(RET_COL0, h)] = _dot(sc.astype(BF16), vh)
                for d in range(2):
                    kw = (kh * wtab_s[li, d, h]).astype(BF16)
                    kv_s[(i * 2 + d) * N_HEADS + h] = _dot_tn(kw, vh)

        def ret_carry():
            for d in range(2):
                for h in range(N_HEADS):
                    hs = slice(h * HEAD_W, (h + 1) * HEAD_W)
                    if latent:
                        state = s_ret_ref[li, d, h]
                    else:
                        state = jnp.zeros((HEAD_W, HEAD_W), F32)
                    g_c = gc_s[li, d, h][0:1, :]
                    order = range(n_tiles) if d == 0 else range(n_tiles - 1, -1, -1)
                    for n in order:
                        rows = pl.ds(n * TM, TM)
                        if ret_carried:
                            qw =(hq_s[rows, hs].astype(F32) * wtab_s[li, 2 + d, h]).astype(BF16)
                            cs = head_cols(RET_COL0, h)
                            dst_ref[rows, cs] = dst_ref[rows, cs] + _dot(qw, state.astype(BF16))
                        state = state * g_c + kv_s[(n * 2 + d) * N_HEADS + h]
                    if not latent:
                        f_ret_ref[layer, d, h] = state

        rtn = rtn_ref[li]

        def ret_out_tile(i):
            rows = rows_of(i)
            head_norm_gate(dst_ref[rows, RET_COL0:RET_COL0 + D_C], rtn, D_A + D_B, rows)

        fg = fg_ref[...]

        def out_tile(i):
            rows = rows_of(i)
            delta = _dot(ycat_s[rows, :], w_out_ref[li])
            out = src_ref[rows, :] + gate * delta
            if last:
                ms = jnp.mean(out * out, axis=-1, keepdims=True)
                out = out * lax.rsqrt(ms + EPS) * fg
            dst_ref[rows, :] = out

        zero_pad = jnp.zeros((8, D_A), F32)
        xpad_s[pl.ds(0, 8), :] = zero_pad
        xpad_s[pl.ds(seq + 8, 8), :] = zero_pad

        def redo_tile_if(unsafe, i):
            @pl.when(unsafe)
            def _():
                hg_tile_direct(i)

        def front_tile(i):
            norm_tile(i)
            lru_proj_tile(i)

        def mixer_tile(i):
            flag_s[i] = hg_tile(i).astype(jnp.int32)
            ret_tile(i)
            lru_gate_tile(i)

        def redo_tile(i):
            redo_tile_if(flag_s[i] == 1, i)

        def scan_chunk(j, carry):
            hg_scan(j)
            for g in range(HG_CHUNK // 8):
                carry = lru_steps(j * (HG_CHUNK // 8) + g, carry)
            return carry

        def back_tile(i):
            if merged:
                lru_out_tile(i)
            hg_out_tile(i)
            ret_out_tile(i)
            out_tile(i)

        _loop(n_tiles, front_tile)
        _loop(n_tiles, mixer_tile)
        if merged:
            _loop(n_tiles, redo_tile)
            hg_state_init()
            carry = (h0f, h0b)
            for j in range(n_chunks):
                carry = scan_chunk(j, carry)
            lru_finish(carry)
        else:
            hf_fin = lax.fori_loop(0, seq // 8, lru_fwd_steps, h0f)
            hb_fin = lax.fori_loop(0, seq // 8, lru_bwd_steps, h0b)
            lru_finish((hf_fin, hb_fin))
            _loop(n_tiles, lru_out_tile)
            _loop(n_tiles, redo_tile)
            hg_state_init()
            _loop(n_chunks, hg_scan, unroll=8)
        hg_finish()
        ret_carry()
        _loop(n_tiles, back_tile)

    for li, layer in enumerate(layers):
        src = x_ref if li == 0 else xmid_s
        dst = y_ref if li == n_local - 1 else xmid_s
        run_layer(li, layer, src, dst)


def _trunk_call(x, mods, layers, latent, prm, states, rope):
    bsz, seq, _ = x.shape
    n_tiles = seq // TM
    n_chunks = seq // HG_CHUNK
    n_local = len(layers)
    l0 = layers[0]
    state_tail = (2, N_HEADS, HEAD_W, HEAD_W)

    def layers_spec(arr):
        tail = arr.shape[1:]
        zeros = (0,) * len(tail)
        return pl.BlockSpec((n_local,) + tail, lambda b: (l0 // n_local,) + zeros, pipeline_mode=pl.Buffered(1))

    def whole_spec(arr):
        zeros = (0,) * arr.ndim
        return pl.BlockSpec(arr.shape, lambda b: zeros, pipeline_mode=pl.Buffered(1))

    def per_seq_layers(tail, n_lay, first):
        zeros = (0,) * len(tail)
        return pl.BlockSpec((None, n_lay) + tail, lambda b: (b, first // n_lay) + zeros)

    act_mode = dict(pipeline_mode=pl.Buffered(1)) if latent else {}
    act_spec = pl.BlockSpec((None, seq, D_MODEL), lambda b: (b, 0, 0), **act_mode)
    cond_row = (lambda b: (l0 // n_local, 1 + b, 0, 0)) if latent else (lambda b: (l0 // n_local, 0, 0, 0))
    mod_spec = pl.BlockSpec((n_local, None, 3, D_MODEL), cond_row)

    stacked = [prm[k] for k in ("norm_g", "w_in", "conv_w", "conv_b", "wblk", "gbias", "lam")]
    operands = [x, mods] + stacked + [prm["lb_logits"], prm["hgrn_norm_g"], prm["rdl"], prm["ret_norm_g"],
                                      prm["w_out"], prm["final_g"]]
    in_specs = ([act_spec, mod_spec] + [layers_spec(a) for a in stacked]
                + [whole_spec(prm["lb_logits"]), layers_spec(prm["hgrn_norm_g"]), layers_spec(prm["rdl"]),
                   layers_spec(prm["ret_norm_g"]), layers_spec(prm["w_out"]), whole_spec(prm["final_g"])])
    if latent:
        operands += list(states) + list(rope)
        in_specs += [per_seq_layers((2, D_A), n_local, l0), per_seq_layers(state_tail, n_local, l0),
                     per_seq_layers(state_tail, n_local, l0)]
        in_specs += [whole_spec(r) for r in rope]
    out_shape = [jax.ShapeDtypeStruct((bsz, seq, D_MODEL), F32)]
    out_specs = [act_spec]
    if not latent:
        assert n_local == DEPTH, "the state outputs are written whole, one call must cover every layer"
        out_shape += [jax.ShapeDtypeStruct((bsz, DEPTH, 2, D_A), F32),
                      jax.ShapeDtypeStruct((bsz, DEPTH) + state_tail, F32),
                      jax.ShapeDtypeStruct((bsz, DEPTH) + state_tail, F32)]
        out_specs += [per_seq_layers((2, D_A), DEPTH, 0), per_seq_layers(state_tail, DEPTH, 0),
                      per_seq_layers(state_tail, DEPTH, 0)]
    scratch = [
        pltpu.VMEM((seq, D_MODEL), BF16),
        pltpu.VMEM((seq, D_MIX), BF16),
        pltpu.VMEM((seq + 16, D_A), F32),
        pltpu.VMEM((seq, D_A), F32), pltpu.VMEM((seq, D_A), F32),
        pltpu.VMEM((seq, D_A), F32), pltpu.VMEM((seq, D_A), F32),
        pltpu.VMEM((seq, D_B), BF16), pltpu.VMEM((seq, D_B), BF16),
        pltpu.VMEM((2, seq, D_B), BF16), pltpu.VMEM((2, seq, D_B), BF16),
        pltpu.VMEM((2 * n_chunks, D_B), F32),
        pltpu.VMEM((2 * N_HEADS, HEAD_W, HEAD_W), F32),
        pltpu.VMEM((n_local, N_HEADS, TM, TM), F32),
        pltpu.VMEM((n_local, 4 if (latent or n_tiles > 1) else 2, N_HEADS, TM, HEAD_W), F32),
        pltpu.VMEM((n_local, 2, N_HEADS, 8, HEAD_W), F32),
        pltpu.VMEM((n_tiles * 2 * N_HEADS, HEAD_W, HEAD_W), F32),
    ]
    if n_local > 1:
        scratch.append(pltpu.VMEM((seq, D_MODEL), F32))
    scratch += [pltpu.SMEM((n_tiles,), jnp.int32)]
    merged = n_chunks <= 8
    if merged:
        scratch += [pltpu.VMEM((seq, D_B), F32)] * 2
    body = functools.partial(_trunk_kernel, seq=seq, layers=tuple(layers), latent=latent, merged=merged)
    return pl.pallas_call(
        body,
        out_shape=out_shape,
        grid=(bsz,),
        in_specs=in_specs,
        out_specs=out_specs,
        scratch_shapes=scratch,
        compiler_params=pltpu.CompilerParams(dimension_semantics=("arbitrary",),
                                             vmem_limit_bytes=V7X_VMEM_LIMIT_BYTES),
        name=("latent" if latent else "context") + "_layers" + "".join(str(l) for l in layers),
    )(*operands)


def _rope_tables(seq):
    half = RET_DK // 2
    nf = half // 2
    t = np.arange(seq)
    freq = (ROPE_BASE ** (-np.arange(nf, dtype=np.float32) / nf)).astype(np.float32)
    ang_r = (t // GRID_W).astype(np.float32)[:, None] * freq[None, :]
    ang_c = (t % GRID_W).astype(np.float32)[:, None] * freq[None, :]
    zero = np.zeros((seq, nf), np.float32)
    cos = np.concatenate([np.cos(ang_r)] * 2 + [np.cos(ang_c)] * 2, axis=-1)
    sin_a = np.concatenate([-np.sin(ang_r), zero, -np.sin(ang_c), zero], axis=-1)
    sin_b = np.concatenate([zero, np.sin(ang_r), zero, np.sin(ang_c)], axis=-1)
    return tuple(jnp.asarray(a, F32) for a in (cos, sin_a, sin_b))


def _block_diag_gates(wa, wx):
    per_half = GATE_HALF // LRU_BW
    n_half = D_A // GATE_HALF
    eye = jnp.eye(per_half, dtype=F32)
    cols = []
    for d in range(2):
        for w in (wa, wx):
            blk = w[:, d].reshape(DEPTH, n_half, per_half, LRU_BW, LRU_BW)
            dense = jnp.einsum('pq,lhpij->lhpiqj', eye, blk).reshape(DEPTH, n_half, GATE_HALF, GATE_HALF)
            cols.append(dense)
    return jnp.concatenate(cols, axis=-1).astype(BF16)


def kernel(x_prompt, x_sample, state_lru, state_hgrn, state_ret, c, c_ctx, norm_g, w_mod, b_mod, w_in, conv_w,
           conv_b, lru_wa, lru_ba, lru_wx, lru_bx, lru_lambda, hgrn_lb_logits, hgrn_norm_g, ret_decay_logit,
           ret_norm_g, w_out, final_g):
    dec_b = x_sample.shape[0]
    pad = jnp.zeros((COND_ROWS - 1 - dec_b, D_MODEL), F32)
    cond = jnp.concatenate([c_ctx[None, :], c, pad], axis=0)
    mods = _modulation(cond, w_mod, b_mod).reshape(DEPTH, COND_ROWS, 3, D_MODEL)

    prm = dict(
        norm_g=norm_g[:, None, :],
        w_in=w_in.astype(BF16),
        conv_w=conv_w,
        conv_b=conv_b[:, None, :],
        wblk=_block_diag_gates(lru_wa, lru_wx),
        gbias=jnp.stack([lru_ba[:, 0], lru_bx[:, 0], lru_ba[:, 1], lru_bx[:, 1]], axis=1),
        lam=lru_lambda,
        lb_logits=hgrn_lb_logits.reshape(DEPTH * 2, D_B),
        hgrn_norm_g=hgrn_norm_g[:, None, :],
        rdl=jnp.broadcast_to(ret_decay_logit.reshape(DEPTH, 2 * RET_HEADS, 1), (DEPTH, 2 * RET_HEADS, TM)),
        ret_norm_g=ret_norm_g[:, None, :],
        w_out=w_out.astype(BF16),
        final_g=final_g[None, :],
    )

    y_prompt, new_lru, new_hgrn, new_ret = _trunk_call(x_prompt, mods, list(range(DEPTH)), False, prm, None, None)

    rope = _rope_tables(x_sample.shape[1])
    z = x_sample
    for l in range(DEPTH):
        (z,) = _trunk_call(z, mods, [l], True, prm, (state_lru, state_hgrn, state_ret), rope)
    y_sample = z

    return (y_prompt, y_sample, new_lru, new_hgrn, new_ret)
```
